```python
import math
import jax, jax.numpy as jnp
from jax import lax
import numpy as np

D_MODEL = 4096
BATCH = 4
SEQ = 2048
DEPTH = 2

N_Q_HEADS = 32
N_KV_HEADS = 8
HEAD_DIM = 64
Q_GROUP = N_Q_HEADS // N_KV_HEADS
ATTN_WIDTH = N_Q_HEADS * HEAD_DIM
KV_WIDTH = N_KV_HEADS * HEAD_DIM
WINDOW = 128
BLOCK = 128
N_BUCKETS = 32
MAX_DISTANCE = 128
CONV_WIDTH = D_MODEL // 4
CONV_KERNEL = 31
SSM_WIDTH = D_MODEL // 4
SSM_GROUP = 16
SSM_GROUPS = SSM_WIDTH // SSM_GROUP
SSM_STATE = 64
D_FF = 256 * ((8 * D_MODEL // 3 + 255) // 256)
N_BRANCH = 3
MIX_WIDTH = ATTN_WIDTH + CONV_WIDTH + SSM_WIDTH
IN_WIDTH = ATTN_WIDTH + 2 * KV_WIDTH + 2 * CONV_WIDTH + SSM_WIDTH + N_BRANCH * D_MODEL
EPS = 1e-6

kernel_name = 'hybrid_gated_parallel_mixer_block'


def rms_norm(x, g):
    xf = x.astype(jnp.float32)
    y = xf * lax.rsqrt(jnp.mean(xf * xf, axis=-1, keepdims=True) + EPS) * g.astype(jnp.float32)
    return y.astype(x.dtype)


def layer_norm(x, g, b):
    xf = x.astype(jnp.float32)
    mu = jnp.mean(xf, axis=-1, keepdims=True)
    var = jnp.mean(jnp.square(xf - mu), axis=-1, keepdims=True)
    y = (xf - mu) * lax.rsqrt(var + EPS) * g.astype(jnp.float32) + b.astype(jnp.float32)
    return y.astype(x.dtype)


def swiglu_ffn(x, w_in, w_out):
    gate, up = jnp.split(x @ w_in, 2, axis=-1)
    return (jax.nn.silu(gate) * up) @ w_out


def t5_bucket(dist):
    max_exact = N_BUCKETS // 2
    n = jnp.maximum(dist, 0)
    ratio = jnp.log(jnp.maximum(n, 1).astype(jnp.float32) / max_exact) / math.log(MAX_DISTANCE / max_exact)
    large = jnp.minimum(max_exact + (ratio * (N_BUCKETS - max_exact)).astype(jnp.int32), N_BUCKETS - 1)
    return jnp.where(n < max_exact, n, large)


def band_bias_and_mask(rel_bias, seq):
    nb = seq // BLOCK
    r = jnp.arange(BLOCK)[:, None]
    c = jnp.arange(2 * BLOCK)[None, :]
    dist = BLOCK + r - c
    band = (dist >= 0) & (dist < WINDOW)
    bias = rel_bias.astype(jnp.float32)[t5_bucket(dist)]
    bias = jnp.transpose(bias, (2, 0, 1)).reshape(N_KV_HEADS, Q_GROUP, BLOCK, 2 * BLOCK)
    blk = jnp.arange(nb)[:, None, None]
    mask = band[None] & ((blk > 0) | (c[None] >= BLOCK))
    return bias, mask


def key_blocks(t, nb):
    b, l = t.shape[0], t.shape[1]
    pad = jnp.pad(t, ((0, 0), (BLOCK, 0), (0, 0), (0, 0)))
    prev = pad[:, :l].reshape(b, nb, BLOCK, N_KV_HEADS, HEAD_DIM)
    cur = pad[:, BLOCK:].reshape(b, nb, BLOCK, N_KV_HEADS, HEAD_DIM)
    return jnp.concatenate([prev, cur], axis=2)


def sliding_window_attention(q, k, v, q_gain, k_gain, sinks, bias, mask):
    b, l, _ = q.shape
    nb = l // BLOCK
    qf = rms_norm(q.astype(jnp.float32).reshape(b, l, N_Q_HEADS, HEAD_DIM), q_gain) * (HEAD_DIM ** -0.5)
    qf = qf.reshape(b, nb, BLOCK, N_KV_HEADS, Q_GROUP, HEAD_DIM)
    kf = rms_norm(k.astype(jnp.float32).reshape(b, l, N_KV_HEADS, HEAD_DIM), k_gain)
    kb = key_blocks(kf, nb)
    vb = key_blocks(v.astype(jnp.float32).reshape(b, l, N_KV_HEADS, HEAD_DIM), nb)
    s = jnp.einsum('bnqhgd,bnkhd->bnhgqk', qf, kb) + bias
    s = jnp.where(mask[None, :, None, None], s, -jnp.inf)
    sink = sinks.astype(jnp.float32).reshape(N_KV_HEADS, Q_GROUP)[:, :, None, None]
    m = jnp.maximum(jnp.max(s, axis=-1, keepdims=True), sink)
    p = jnp.exp(s - m)
    p = p / (jnp.sum(p, axis=-1, keepdims=True) + jnp.exp(sink - m))
    o = jnp.einsum('bnhgqk,bnkhd->bnqhgd', p, vb)
    return o.reshape(b, l, ATTN_WIDTH).astype(q.dtype)


def conformer_conv(u2, w_dw, b_dw, ln_g, ln_b, w_pw):
    a, g = jnp.split(u2, 2, axis=-1)
    u = a * jax.nn.sigmoid(g)
    y = lax.conv_general_dilated(u, w_dw[:, None, :].astype(u.dtype), (1,), [(CONV_KERNEL - 1, 0)],
                                 dimension_numbers=('NWC', 'WIO', 'NWC'),
                                 feature_group_count=CONV_WIDTH) + b_dw
    y = jax.nn.silu(layer_norm(y, ln_g, ln_b))
    return y @ w_pw


def s5_ssm(u, a_re, a_im, log_dt, b_re, b_im, c_re, c_im, d_skip, w_glu):
    b, l, _ = u.shape
    f32 = jnp.float32
    uf = u.astype(f32).reshape(b, l, SSM_GROUPS, SSM_GROUP)
    dt = jnp.exp(log_dt.astype(f32))[:, None]
    lam_re = a_re.astype(f32)
    lam_im = a_im.astype(f32)
    mag = jnp.exp(dt * lam_re)
    ang = dt * lam_im
    lb_re = mag * jnp.cos(ang)
    lb_im = mag * jnp.sin(ang)
    nr = lb_re - 1.0
    den = lam_re * lam_re + lam_im * lam_im
    coef_re = (nr * lam_re + lb_im * lam_im) / den
    coef_im = (lb_im * lam_re - nr * lam_im) / den
    br = b_re.astype(f32)
    bi = b_im.astype(f32)
    bb_re = coef_re[..., None] * br - coef_im[..., None] * bi
    bb_im = coef_re[..., None] * bi + coef_im[..., None] * br
    bu_re = jnp.einsum('blgc,gpc->blgp', uf, bb_re)
    bu_im = jnp.einsum('blgc,gpc->blgp', uf, bb_im)
    at_re = jnp.broadcast_to(lb_re, bu_re.shape)
    at_im = jnp.broadcast_to(lb_im, bu_im.shape)

    def combine(e1, e2):
        ar1, ai1, br1, bi1 = e1
        ar2, ai2, br2, bi2 = e2
        return (ar1 * ar2 - ai1 * ai2,
                ar1 * ai2 + ai1 * ar2,
                ar2 * br1 - ai2 * bi1 + br2,
                ar2 * bi1 + ai2 * br1 + bi2)

    _, _, h_re, h_im = lax.associative_scan(combine, (at_re, at_im, bu_re, bu_im), axis=1)
    y = (jnp.einsum('blgp,gcp->blgc', h_re, c_re.astype(f32))
         - jnp.einsum('blgp,gcp->blgc', h_im, c_im.astype(f32)))
    y = y.reshape(b, l, SSM_WIDTH) + d_skip.astype(f32) * uf.reshape(b, l, SSM_WIDTH)
    y = jax.nn.gelu(y).astype(u.dtype)
    za, zb = jnp.split(y @ w_glu, 2, axis=-1)
    return za * jax.nn.sigmoid(zb)


def hybrid_layer(x, bias, mask, ffn1_norm, w_ffn1_in, w_ffn1_out, mix_norm, w_in, q_norm, k_norm,
                 attn_sinks, conv_dw, conv_dw_bias, conv_ln_g, conv_ln_b, conv_pw,
                 ssm_a_re, ssm_a_im, ssm_log_dt, ssm_b_re, ssm_b_im, ssm_c_re, ssm_c_im, ssm_d, ssm_glu,
                 w_branch, w_out, ffn2_norm, w_ffn2_in, w_ffn2_out):
    b, l, _ = x.shape
    x = x + 0.5 * swiglu_ffn(rms_norm(x, ffn1_norm), w_ffn1_in, w_ffn1_out)
    h = rms_norm(x, mix_norm)
    proj = h @ w_in
    o0 = ATTN_WIDTH
    o1 = o0 + KV_WIDTH
    o2 = o1 + KV_WIDTH
    o3 = o2 + 2 * CONV_WIDTH
    o4 = o3 + SSM_WIDTH
    q, k, v = proj[..., :o0], proj[..., o0:o1], proj[..., o1:o2]
    conv_in, ssm_in, gate_logits = proj[..., o2:o3], proj[..., o3:o4], proj[..., o4:]
    o_attn = sliding_window_attention(q, k, v, q_norm, k_norm, attn_sinks, bias, mask)
    o_conv = conformer_conv(conv_in, conv_dw, conv_dw_bias, conv_ln_g, conv_ln_b, conv_pw)
    o_ssm = s5_ssm(ssm_in, ssm_a_re, ssm_a_im, ssm_log_dt, ssm_b_re, ssm_b_im, ssm_c_re, ssm_c_im,
                   ssm_d, ssm_glu)
    y_attn = o_attn @ w_branch[:ATTN_WIDTH]
    y_conv = o_conv @ w_branch[ATTN_WIDTH:ATTN_WIDTH + CONV_WIDTH]
    y_ssm = o_ssm @ w_branch[ATTN_WIDTH + CONV_WIDTH:]
    gates = jax.nn.sigmoid(gate_logits.astype(jnp.float32)).reshape(b, l, N_BRANCH, D_MODEL)
    merged = gates[:, :, 0] * y_attn + gates[:, :, 1] * y_conv + gates[:, :, 2] * y_ssm
    x = x + merged.astype(x.dtype) @ w_out
    x = x + 0.5 * swiglu_ffn(rms_norm(x, ffn2_norm), w_ffn2_in, w_ffn2_out)
    return x


def setup_inputs(seed: int = 0) -> dict:
    key = jax.random.key(seed)
    ks = jax.random.split(key, 32)
    f32 = jnp.float32

    def nrm(k, shape, scale):
        return jax.random.normal(k, shape, f32) * scale

    a_im_init = jnp.pi * jnp.arange(SSM_STATE, dtype=f32)
    return {
        'x': nrm(ks[0], (BATCH, SEQ, D_MODEL), 1.0),
        'rel_bias': nrm(ks[1], (N_BUCKETS, N_Q_HEADS), 0.5),
        'ffn1_norm': 1.0 + nrm(ks[2], (DEPTH, D_MODEL), 0.02),
        'w_ffn1_in': nrm(ks[3], (DEPTH, D_MODEL, 2 * D_FF), D_MODEL ** -0.5),
        'w_ffn1_out': nrm(ks[4], (DEPTH, D_FF, D_MODEL), D_FF ** -0.5),
        'mix_norm': 1.0 + nrm(ks[5], (DEPTH, D_MODEL), 0.02),
        'w_in': nrm(ks[6], (DEPTH, D_MODEL, IN_WIDTH), D_MODEL ** -0.5),
        'q_norm': 1.0 + nrm(ks[7], (DEPTH, HEAD_DIM), 0.02),
        'k_norm': 1.0 + nrm(ks[8], (DEPTH, HEAD_DIM), 0.02),
        'attn_sinks': nrm(ks[9], (DEPTH, N_Q_HEADS), 1.0),
        'conv_dw': nrm(ks[10], (DEPTH, CONV_KERNEL, CONV_WIDTH), CONV_KERNEL ** -0.5),
        'conv_dw_bias': nrm(ks[11], (DEPTH, CONV_WIDTH), 0.02),
        'conv_ln_g': 1.0 + nrm(ks[12], (DEPTH, CONV_WIDTH), 0.02),
        'conv_ln_b': nrm(ks[13], (DEPTH, CONV_WIDTH), 0.02),
        'conv_pw': nrm(ks[14], (DEPTH, CONV_WIDTH, CONV_WIDTH), CONV_WIDTH ** -0.5),
        'ssm_a_re': -0.5 + nrm(ks[15], (DEPTH, SSM_GROUPS, SSM_STATE), 0.01),
        'ssm_a_im': a_im_init + nrm(ks[16], (DEPTH, SSM_GROUPS, SSM_STATE), 0.01),
        'ssm_log_dt': jax.random.uniform(ks[17], (DEPTH, SSM_GROUPS), f32,
                                         minval=math.log(0.001), maxval=math.log(0.1)),
        'ssm_b_re': nrm(ks[18], (DEPTH, SSM_GROUPS, SSM_STATE, SSM_GROUP), (2 * SSM_GROUP) ** -0.5),
        'ssm_b_im': nrm(ks[19], (DEPTH, SSM_GROUPS, SSM_STATE, SSM_GROUP), (2 * SSM_GROUP) ** -0.5),
        'ssm_c_re': nrm(ks[20], (DEPTH, SSM_GROUPS, SSM_GROUP, SSM_STATE), (2 * SSM_STATE) ** -0.5),
        'ssm_c_im': nrm(ks[21], (DEPTH, SSM_GROUPS, SSM_GROUP, SSM_STATE), (2 * SSM_STATE) ** -0.5),
        'ssm_d': nrm(ks[22], (DEPTH, SSM_WIDTH), 1.0),
        'ssm_glu': nrm(ks[23], (DEPTH, SSM_WIDTH, 2 * SSM_WIDTH), SSM_WIDTH ** -0.5),
        'w_branch': nrm(ks[24], (DEPTH, MIX_WIDTH, D_MODEL), CONV_WIDTH ** -0.5),
        'w_out': nrm(ks[25], (DEPTH, D_MODEL, D_MODEL), D_MODEL ** -0.5),
        'ffn2_norm': 1.0 + nrm(ks[26], (DEPTH, D_MODEL), 0.02),
        'w_ffn2_in': nrm(ks[27], (DEPTH, D_MODEL, 2 * D_FF), D_MODEL ** -0.5),
        'w_ffn2_out': nrm(ks[28], (DEPTH, D_FF, D_MODEL), D_FF ** -0.5),
    }


def reference(x, rel_bias, ffn1_norm, w_ffn1_in, w_ffn1_out, mix_norm, w_in, q_norm, k_norm,
              attn_sinks, conv_dw, conv_dw_bias, conv_ln_g, conv_ln_b, conv_pw,
              ssm_a_re, ssm_a_im, ssm_log_dt, ssm_b_re, ssm_b_im, ssm_c_re, ssm_c_im, ssm_d, ssm_glu,
              w_branch, w_out, ffn2_norm, w_ffn2_in, w_ffn2_out):
    bias, mask = band_bias_and_mask(rel_bias, x.shape[1])
    for i in range(DEPTH):
        x = hybrid_layer(x, bias, mask, ffn1_norm[i], w_ffn1_in[i], w_ffn1_out[i], mix_norm[i], w_in[i],
                         q_norm[i], k_norm[i], attn_sinks[i], conv_dw[i], conv_dw_bias[i], conv_ln_g[i],
                         conv_ln_b[i], conv_pw[i], ssm_a_re[i], ssm_a_im[i], ssm_log_dt[i], ssm_b_re[i],
                         ssm_b_im[i], ssm_c_re[i], ssm_c_im[i], ssm_d[i], ssm_glu[i], w_branch[i], w_out[i],
                         ffn2_norm[i], w_ffn2_in[i], w_ffn2_out[i])
    return x
```

```python
import functools
import math

import jax
import jax.numpy as jnp
from jax import lax
from jax.experimental import pallas as pl
from jax.experimental.pallas import tpu as pltpu

F32 = jnp.float32
BF16 = jnp.bfloat16

D_MODEL = 4096
DEPTH = 2
N_Q_HEADS = 32
N_KV_HEADS = 8
HEAD_DIM = 64
Q_GROUP = N_Q_HEADS // N_KV_HEADS
ATTN_WIDTH = N_Q_HEADS * HEAD_DIM
KV_WIDTH = N_KV_HEADS * HEAD_DIM
WINDOW = 128
BLOCK = 128
N_BUCKETS = 32
MAX_DISTANCE = 128
CONV_WIDTH = D_MODEL // 4
CONV_KERNEL = 31
SSM_WIDTH = D_MODEL // 4
SSM_GROUP = 16
SSM_GROUPS = SSM_WIDTH // SSM_GROUP
SSM_STATE = 64
D_FF = 256 * ((8 * D_MODEL // 3 + 255) // 256)
N_BRANCH = 3
MIX_WIDTH = ATTN_WIDTH + CONV_WIDTH + SSM_WIDTH
IN_WIDTH = ATTN_WIDTH + 2 * KV_WIDTH + 2 * CONV_WIDTH + SSM_WIDTH + N_BRANCH * D_MODEL
EPS = 1e-6

OFF_K = ATTN_WIDTH
OFF_V = OFF_K + KV_WIDTH
OFF_CONV = OFF_V + KV_WIDTH
OFF_SSM = OFF_CONV + 2 * CONV_WIDTH
OFF_GATE = OFF_SSM + SSM_WIDTH

V7X_LANES = 128
V7X_VMEM_BYTES = 64 * 1024 * 1024
V7X_VMEM_INTERNAL_BYTES = 12 * 1024 * 1024

SSM_CHUNK = 8
SSM_QGROUPS = 8
SSM_QCH = SSM_QGROUPS * SSM_GROUP
SSM_NQ = SSM_GROUPS // SSM_QGROUPS
SSM_QSTATE = SSM_QGROUPS * SSM_STATE


def _nbytes(shape, dtype):
    return math.prod(shape) * jnp.dtype(dtype).itemsize


def _params(block_bytes, scratch_bytes=0, semantics=None):
    limit = 2 * block_bytes + scratch_bytes + V7X_VMEM_INTERNAL_BYTES
    limit = min(limit, V7X_VMEM_BYTES - 4 * 1024 * 1024)
    return pltpu.CompilerParams(dimension_semantics=semantics, vmem_limit_bytes=int(limit))


def _rmsnorm_kernel(x_ref, g_ref, o_ref):
    x = x_ref[...]
    ms = jnp.mean(x * x, axis=-1, keepdims=True)
    o_ref[...] = (x * lax.rsqrt(ms + EPS) * g_ref[...]).astype(o_ref.dtype)


def rmsnorm(x, g, bm=256):
    m, d = x.shape
    blk = _nbytes((bm, d), F32) + _nbytes((bm, d), BF16) + _nbytes((1, d), F32)
    return pl.pallas_call(
        _rmsnorm_kernel,
        grid=(m // bm,),
        in_specs=[pl.BlockSpec((bm, d), lambda i: (i, 0)),
                  pl.BlockSpec((1, d), lambda i: (0, 0))],
        out_specs=pl.BlockSpec((bm, d), lambda i: (i, 0)),
        out_shape=jax.ShapeDtypeStruct((m, d), BF16),
        compiler_params=_params(blk, semantics=("arbitrary",)),
        name="rmsnorm",
    )(x, g.reshape(1, d))


def _swiglu_kernel(h_ref, wg_ref, wu_ref, o_ref):
    h = h_ref[...]
    g = jnp.dot(h, wg_ref[...], preferred_element_type=F32)
    u = jnp.dot(h, wu_ref[...], preferred_element_type=F32)
    o_ref[...] = (g * jax.nn.sigmoid(g) * u).astype(o_ref.dtype)


def swiglu_in(h, w, bm=1024, bn=256):
    m, k = h.shape
    f = w.shape[1] // 2
    bm = min(bm, m)
    nj = f // bn
    blk = (_nbytes((bm, k), BF16) + 2 * _nbytes((k, bn), BF16) + _nbytes((bm, bn), BF16))
    return pl.pallas_call(
        _swiglu_kernel,
        grid=(m // bm, nj),
        in_specs=[pl.BlockSpec((bm, k), lambda i, j: (i, 0)),
                  pl.BlockSpec((k, bn), lambda i, j: (0, j)),
                  pl.BlockSpec((k, bn), lambda i, j: (0, j + nj))],
        out_specs=pl.BlockSpec((bm, bn), lambda i, j: (i, j)),
        out_shape=jax.ShapeDtypeStruct((m, f), BF16),
        compiler_params=_params(blk, semantics=("arbitrary", "arbitrary")),
        name="swiglu_in",
    )(h, w, w)


def _matmul_kernel(a_ref, w_ref, o_ref):
    o_ref[...] = jnp.dot(a_ref[...], w_ref[...], preferred_element_type=F32).astype(o_ref.dtype)


def matmul(a, w, out_dtype, bm=1024, bn=1024):
    m, k = a.shape
    n = w.shape[1]
    bm, bn = min(bm, m), min(bn, n)
    blk = _nbytes((bm, k), a.dtype) + _nbytes((k, bn), w.dtype) + _nbytes((bm, bn), out_dtype)
    return pl.pallas_call(
        _matmul_kernel,
        grid=(m // bm, n // bn),
        in_specs=[pl.BlockSpec((bm, k), lambda i, j: (i, 0)),
                  pl.BlockSpec((k, bn), lambda i, j: (0, j))],
        out_specs=pl.BlockSpec((bm, bn), lambda i, j: (i, j)),
        out_shape=jax.ShapeDtypeStruct((m, n), out_dtype),
        compiler_params=_params(blk, semantics=("arbitrary", "arbitrary")),
        name="matmul",
    )(a, w)


def _matmul_res_kernel(a_ref, w_ref, x_ref, o_ref, *, scale):
    y = jnp.dot(a_ref[...], w_ref[...], preferred_element_type=F32)
    o_ref[...] = x_ref[...] + scale * y


def matmul_res(a, w, x, scale, bm=512, bn=512):
    m, k = a.shape
    n = w.shape[1]
    bm, bn = min(bm, m), min(bn, n)
    blk = (_nbytes((bm, k), a.dtype) + _nbytes((k, bn), w.dtype) + 2 * _nbytes((bm, bn), F32))
    return pl.pallas_call(
        functools.partial(_matmul_res_kernel, scale=scale),
        grid=(m // bm, n // bn),
        in_specs=[pl.BlockSpec((bm, k), lambda i, j: (i, 0)),
                  pl.BlockSpec((k, bn), lambda i, j: (0, j)),
                  pl.BlockSpec((bm, bn), lambda i, j: (i, j))],
        out_specs=pl.BlockSpec((bm, bn), lambda i, j: (i, j)),
        out_shape=jax.ShapeDtypeStruct((m, n), F32),
        compiler_params=_params(blk, semantics=("arbitrary", "arbitrary")),
        name="matmul_res",
    )(a, w, x)


def _glu_kernel(a_ref, wa_ref, wb_ref, o_ref):
    a = a_ref[...].astype(BF16)
    za = jnp.dot(a, wa_ref[...], preferred_element_type=F32)
    zb = jnp.dot(a, wb_ref[...], preferred_element_type=F32)
    o_ref[...] = (za * jax.nn.sigmoid(zb)).astype(o_ref.dtype)


def glu_matmul(a, w, bm=1024, bn=256):
    m, k = a.shape
    n = w.shape[1] // 2
    bm = min(bm, m)
    nj = n // bn
    blk = _nbytes((bm, k), a.dtype) + 2 * _nbytes((k, bn), BF16) + _nbytes((bm, bn), BF16)
    return pl.pallas_call(
        _glu_kernel,
        grid=(m // bm, nj),
        in_specs=[pl.BlockSpec((bm, k), lambda i, j: (i, 0)),
                  pl.BlockSpec((k, bn), lambda i, j: (0, j)),
                  pl.BlockSpec((k, bn), lambda i, j: (0, j + nj))],
        out_specs=pl.BlockSpec((bm, bn), lambda i, j: (i, j)),
        out_shape=jax.ShapeDtypeStruct((m, n), BF16),
        compiler_params=_params(blk, semantics=("arbitrary", "arbitrary")),
        name="glu_matmul",
    )(a, w, w)


def _segment_mean_sq(x, seg_ref):
    s = jnp.dot(x * x, seg_ref[...], preferred_element_type=F32, precision=lax.Precision.HIGHEST)
    return s * (1.0 / HEAD_DIM)


def _attn_kernel(sink_ref, q_ref, kp_ref, kc_ref, vp_ref, vc_ref, bias_ref, qg_ref, kg_ref,
                 seg_ref, o_ref, *, nb):
    n = pl.program_id(0) % nb
    lanes = V7X_LANES

    kk = jnp.concatenate([kp_ref[...], kc_ref[...]], axis=0)
    vv = jnp.concatenate([vp_ref[...], vc_ref[...]], axis=0)
    ks = []
    for c in range(KV_WIDTH // lanes):
        kc = kk[:, c * lanes:(c + 1) * lanes]
        ks.append(kc * lax.rsqrt(_segment_mean_sq(kc, seg_ref) + EPS))
    kn = jnp.concatenate(ks, axis=1) * kg_ref[...]

    r = lax.broadcasted_iota(jnp.int32, (BLOCK, 2 * BLOCK), 0)
    c = lax.broadcasted_iota(jnp.int32, (BLOCK, 2 * BLOCK), 1)
    dist = BLOCK + r - c
    mask = (dist >= 0) & (dist < WINDOW) & ((n > 0) | (c >= BLOCK))
    low = lax.broadcasted_iota(jnp.int32, (1, lanes), 1) < HEAD_DIM

    for pair in range(ATTN_WIDTH // lanes):
        kv = pair // (Q_GROUP // 2)
        kvp, odd = kv // 2, kv % 2
        kpair = kn[:, kvp * lanes:(kvp + 1) * lanes]
        vpair = vv[:, kvp * lanes:(kvp + 1) * lanes]
        kswap = pltpu.roll(kpair, HEAD_DIM, axis=1)
        vswap = pltpu.roll(vpair, HEAD_DIM, axis=1)
        kdup = (jnp.where(low, kswap, kpair) if odd else jnp.where(low, kpair, kswap)).astype(BF16)
        vdup = (jnp.where(low, vswap, vpair) if odd else jnp.where(low, vpair, vswap)).astype(BF16)

        q2 = q_ref[:, pair * lanes:(pair + 1) * lanes]
        q2 = q2 * lax.rsqrt(_segment_mean_sq(q2, seg_ref) + EPS)
        q2 = q2 * qg_ref[:, pair * lanes:(pair + 1) * lanes] * (HEAD_DIM ** -0.5)
        outs = []
        for half in range(2):
            head = 2 * pair + half
            sel = low if half == 0 else jnp.logical_not(low)
            qh = jnp.where(sel, q2, 0.0).astype(BF16)
            s = lax.dot_general(qh, kdup, (((1,), (1,)), ((), ())), preferred_element_type=F32)
            s = s + bias_ref[head]
            s = jnp.where(mask, s, -jnp.inf)
            sink = sink_ref[head]
            mx = jnp.maximum(jnp.max(s, axis=-1, keepdims=True), sink)
            p = jnp.exp(s - mx)
            den = jnp.sum(p, axis=-1, keepdims=True) + jnp.exp(sink - mx)
            o = jnp.dot(p.astype(BF16), vdup, preferred_element_type=F32)
            outs.append(o / den)
        o_ref[:, pair * lanes:(pair + 1) * lanes] = jnp.where(low, outs[0], outs[1]).astype(o_ref.dtype)


def attention(proj, bias, q_gain, k_gain, sinks, seq):
    m = proj.shape[0]
    nb = seq // BLOCK
    nblk = m // BLOCK
    lanes = V7X_LANES
    seg_id = jnp.arange(lanes) // HEAD_DIM
    seg = (seg_id[:, None] == seg_id[None, :]).astype(F32)
    qg = jnp.tile(q_gain.astype(F32), N_Q_HEADS).reshape(1, ATTN_WIDTH)
    kg = jnp.tile(k_gain.astype(F32), N_KV_HEADS).reshape(1, KV_WIDTH)
    kcol, vcol = OFF_K // KV_WIDTH, OFF_V // KV_WIDTH

    def prev(i):
        return jnp.where(i % nb == 0, i, i - 1)

    blk = (_nbytes((BLOCK, ATTN_WIDTH), F32) + 4 * _nbytes((BLOCK, KV_WIDTH), F32)
           + _nbytes(bias.shape, F32) + _nbytes((BLOCK, ATTN_WIDTH), BF16))
    return pl.pallas_call(
        functools.partial(_attn_kernel, nb=nb),
        grid=(nblk,),
        in_specs=[pl.BlockSpec(memory_space=pltpu.SMEM),
                  pl.BlockSpec((BLOCK, ATTN_WIDTH), lambda i: (i, 0)),
                  pl.BlockSpec((BLOCK, KV_WIDTH), lambda i: (prev(i), kcol)),
                  pl.BlockSpec((BLOCK, KV_WIDTH), lambda i: (i, kcol)),
                  pl.BlockSpec((BLOCK, KV_WIDTH), lambda i: (prev(i), vcol)),
                  pl.BlockSpec((BLOCK, KV_WIDTH), lambda i: (i, vcol)),
                  pl.BlockSpec(bias.shape, lambda i: (0, 0, 0)),
                  pl.BlockSpec((1, ATTN_WIDTH), lambda i: (0, 0)),
                  pl.BlockSpec((1, KV_WIDTH), lambda i: (0, 0)),
                  pl.BlockSpec((lanes, lanes), lambda i: (0, 0))],
        out_specs=pl.BlockSpec((BLOCK, ATTN_WIDTH), lambda i: (i, 0)),
        out_shape=jax.ShapeDtypeStruct((m, ATTN_WIDTH), BF16),
        compiler_params=_params(blk, semantics=("arbitrary",)),
        name="swa_attention",
    )(sinks.astype(F32), proj, proj, proj, proj, proj, bias, qg, kg, seg)


def _t5_bucket(dist):
    max_exact = N_BUCKETS // 2
    n = jnp.maximum(dist, 0)
    ratio = jnp.log(jnp.maximum(n, 1).astype(F32) / max_exact) / math.log(MAX_DISTANCE / max_exact)
    large = jnp.minimum(max_exact + (ratio * (N_BUCKETS - max_exact)).astype(jnp.int32), N_BUCKETS - 1)
    return jnp.where(n < max_exact, n, large)


def band_bias(rel_bias):
    r = jnp.arange(BLOCK)[:, None]
    c = jnp.arange(2 * BLOCK)[None, :]
    bias = rel_bias.astype(F32)[_t5_bucket(BLOCK + r - c)]
    return jnp.transpose(bias, (2, 0, 1))


CONV_HALO = 32
CONV_ROWS = 32


def _conv_kernel(a_ref, g_ref, w_ref, b_ref, lg_ref, lb_ref, pw_ref, o_ref, ubuf, ybuf, *, nt, tt):
    t = pl.program_id(0) % nt

    @pl.when(t == 0)
    def _():
        ubuf[0:CONV_HALO, :] = jnp.zeros((CONV_HALO, CONV_WIDTH), F32)

    @pl.when(t != 0)
    def _():
        ubuf[0:CONV_HALO, :] = ubuf[tt:tt + CONV_HALO, :]

    ubuf[CONV_HALO:CONV_HALO + tt, :] = a_ref[...] * jax.nn.sigmoid(g_ref[...])

    first = CONV_HALO - (CONV_KERNEL - 1)
    cw = 2 * V7X_LANES

    def rows(rb, carry):
        base = pl.multiple_of(rb * CONV_ROWS, CONV_ROWS)
        for cb in range(CONV_WIDTH // cw):
            win = ubuf[pl.ds(base, CONV_ROWS + CONV_HALO), cb * cw:(cb + 1) * cw]
            acc = jnp.zeros((CONV_ROWS, cw), F32) + b_ref[:, cb * cw:(cb + 1) * cw]
            for j in range(CONV_KERNEL):
                acc = acc + w_ref[j:j + 1, cb * cw:(cb + 1) * cw] * win[first + j:first + j + CONV_ROWS, :]
            ybuf[pl.ds(base, CONV_ROWS), cb * cw:(cb + 1) * cw] = acc
        return carry

    lax.fori_loop(0, tt // CONV_ROWS, rows, 0)

    y = ybuf[...]
    mu = jnp.mean(y, axis=-1, keepdims=True)
    yc = y - mu
    var = jnp.mean(yc * yc, axis=-1, keepdims=True)
    z = yc * lax.rsqrt(var + EPS) * lg_ref[...] + lb_ref[...]
    z = z * jax.nn.sigmoid(z)
    o_ref[...] = jnp.dot(z.astype(BF16), pw_ref[...], preferred_element_type=F32).astype(o_ref.dtype)


def conv_branch(proj, w_dw, b_dw, ln_g, ln_b, w_pw, seq, tt=256):
    m = proj.shape[0]
    tt = min(tt, seq)
    nt = seq // tt
    acol = OFF_CONV // CONV_WIDTH
    w_pad = jnp.zeros((CONV_HALO, CONV_WIDTH), F32).at[:CONV_KERNEL].set(w_dw.astype(F32))
    row = lambda v: v.astype(F32).reshape(1, CONV_WIDTH)
    blk = (2 * _nbytes((tt, CONV_WIDTH), F32) + _nbytes((CONV_HALO, CONV_WIDTH), F32)
           + 3 * _nbytes((1, CONV_WIDTH), F32) + _nbytes((CONV_WIDTH, CONV_WIDTH), BF16)
           + _nbytes((tt, CONV_WIDTH), BF16))
    scratch = _nbytes((tt + CONV_HALO, CONV_WIDTH), F32) + _nbytes((tt, CONV_WIDTH), F32)
    return pl.pallas_call(
        functools.partial(_conv_kernel, nt=nt, tt=tt),
        grid=(m // tt,),
        in_specs=[pl.BlockSpec((tt, CONV_WIDTH), lambda i: (i, acol)),
                  pl.BlockSpec((tt, CONV_WIDTH), lambda i: (i, acol + 1)),
                  pl.BlockSpec((CONV_HALO, CONV_WIDTH), lambda i: (0, 0)),
                  pl.BlockSpec((1, CONV_WIDTH), lambda i: (0, 0)),
                  pl.BlockSpec((1, CONV_WIDTH), lambda i: (0, 0)),
                  pl.BlockSpec((1, CONV_WIDTH), lambda i: (0, 0)),
                  pl.BlockSpec((CONV_WIDTH, CONV_WIDTH), lambda i: (0, 0))],
        out_specs=pl.BlockSpec((tt, CONV_WIDTH), lambda i: (i, 0)),
        out_shape=jax.ShapeDtypeStruct((m, CONV_WIDTH), BF16),
        scratch_shapes=[pltpu.VMEM((tt + CONV_HALO, CONV_WIDTH), F32),
                        pltpu.VMEM((tt, CONV_WIDTH), F32)],
        compiler_params=_params(blk, scratch, semantics=("arbitrary",)),
        name="conv_branch",
    )(proj, proj, w_pad, row(b_dw), row(ln_g), row(ln_b), w_pw)


def _ssm_kernel(u_ref, ws_ref, wi_ref, wx_ref, are_ref, aim_ref, d_ref, o_ref, *, nc):
    x = jnp.concatenate([u_ref[pl.ds(t, nc, stride=SSM_CHUNK), :] for t in range(SSM_CHUNK)], axis=1)
    xb = x.astype(BF16)
    s = jnp.dot(xb, ws_ref[0], preferred_element_type=F32)
    h_re, h_im = s[:, :SSM_QSTATE], s[:, SSM_QSTATE:]
    row = lax.broadcasted_iota(jnp.int32, (nc, 1), 0)
    level = 0
    while (1 << level) < nc:
        d = 1 << level
        keep = row >= d
        p_re = jnp.where(keep, pltpu.roll(h_re, d, axis=0), 0.0)
        p_im = jnp.where(keep, pltpu.roll(h_im, d, axis=0), 0.0)
        a_re = are_ref[0, level:level + 1, :]
        a_im = aim_ref[0, level:level + 1, :]
        h_re, h_im = (h_re + a_re * p_re - a_im * p_im,
                      h_im + a_re * p_im + a_im * p_re)
        level += 1
    keep = row >= 1
    hp = jnp.concatenate([jnp.where(keep, pltpu.roll(h_re, 1, axis=0), 0.0),
                          jnp.where(keep, pltpu.roll(h_im, 1, axis=0), 0.0)], axis=1)
    y = (jnp.dot(xb, wi_ref[0], preferred_element_type=F32)
         + jnp.dot(hp.astype(BF16), wx_ref[0], preferred_element_type=F32)
         + d_ref[0] * x)
    y = jax.nn.gelu(y)
    for t in range(SSM_CHUNK):
        o_ref[pl.ds(t, nc, stride=SSM_CHUNK), :] = y[:, t * SSM_QCH:(t + 1) * SSM_QCH]


def ssm_branch(proj, w_state, w_intra, w_inter, a_re, a_im, d_skip, seq):
    m = proj.shape[0]
    nseq = m // seq
    nc = seq // SSM_CHUNK
    ucol = OFF_SSM // SSM_QCH
    kw = SSM_CHUNK * SSM_QCH
    nlev = a_re.shape[1]
    d_t = jnp.tile(d_skip.astype(F32).reshape(SSM_NQ, 1, SSM_QCH), (1, 1, SSM_CHUNK))
    blk = (2 * _nbytes((seq, SSM_QCH), F32) + 3 * _nbytes((kw, kw), BF16)
           + 2 * _nbytes((nlev, SSM_QSTATE), F32) + _nbytes((1, kw), F32))
    return pl.pallas_call(
        functools.partial(_ssm_kernel, nc=nc),
        grid=(SSM_NQ, nseq),
        in_specs=[pl.BlockSpec((seq, SSM_QCH), lambda q, b: (b, ucol + q)),
                  pl.BlockSpec((1, kw, 2 * SSM_QSTATE), lambda q, b: (q, 0, 0)),
                  pl.BlockSpec((1, kw, kw), lambda q, b: (q, 0, 0)),
                  pl.BlockSpec((1, 2 * SSM_QSTATE, kw), lambda q, b: (q, 0, 0)),
                  pl.BlockSpec((1, nlev, SSM_QSTATE), lambda q, b: (q, 0, 0)),
                  pl.BlockSpec((1, nlev, SSM_QSTATE), lambda q, b: (q, 0, 0)),
                  pl.BlockSpec((1, 1, kw), lambda q, b: (q, 0, 0))],
        out_specs=pl.BlockSpec((seq, SSM_QCH), lambda q, b: (b, q)),
        out_shape=jax.ShapeDtypeStruct((m, SSM_WIDTH), F32),
        compiler_params=_params(blk, semantics=("arbitrary", "arbitrary")),
        name="ssm_scan",
    )(proj, w_state, w_intra, w_inter, a_re, a_im, d_t)


def _cmul(ar, ai, br, bi):
    return ar * br - ai * bi, ar * bi + ai * br


def ssm_operands(a_re, a_im, log_dt, b_re, b_im, c_re, c_im, n_chunks):
    f32 = F32
    dt = jnp.exp(log_dt.astype(f32))[:, None]
    lam_re, lam_im = a_re.astype(f32), a_im.astype(f32)
    mag = jnp.exp(dt * lam_re)
    ang = dt * lam_im
    lb_re, lb_im = mag * jnp.cos(ang), mag * jnp.sin(ang)
    nr = lb_re - 1.0
    den = lam_re * lam_re + lam_im * lam_im
    coef_re = (nr * lam_re + lb_im * lam_im) / den
    coef_im = (lb_im * lam_re - nr * lam_im) / den
    br, bi = b_re.astype(f32), b_im.astype(f32)
    bb_re = coef_re[..., None] * br - coef_im[..., None] * bi
    bb_im = coef_re[..., None] * bi + coef_im[..., None] * br
    cr, ci = c_re.astype(f32), c_im.astype(f32)

    pw = [(jnp.ones_like(lb_re), jnp.zeros_like(lb_im))]
    for _ in range(SSM_CHUNK):
        pw.append(_cmul(pw[-1][0], pw[-1][1], lb_re, lb_im))

    T, Q, C, P = SSM_CHUNK, SSM_QGROUPS, SSM_GROUP, SSM_STATE
    eye_q = jnp.eye(Q, dtype=f32)
    grp = lambda v: v.reshape((SSM_NQ, Q) + v.shape[1:])

    ws = []
    for s in range(T):
        e_re, e_im = _cmul(pw[T - 1 - s][0][..., None], pw[T - 1 - s][1][..., None], bb_re, bb_im)
        blocks = jnp.stack([grp(e_re), grp(e_im)], axis=2)
        ws.append(jnp.einsum('nqrpc,qh->nqcrhp', blocks, eye_q))
    w_state = jnp.stack(ws, axis=1).reshape(SSM_NQ, T * Q * C, 2 * Q * P)

    ker = []
    for l in range(T):
        e_re, e_im = _cmul(pw[l][0][..., None], pw[l][1][..., None], bb_re, bb_im)
        ker.append(jnp.einsum('gdp,gpc->gdc', cr, e_re, precision=lax.Precision.HIGHEST)
                   - jnp.einsum('gdp,gpc->gdc', ci, e_im, precision=lax.Precision.HIGHEST))
    zero = jnp.zeros_like(ker[0])
    rows = []
    for s in range(T):
        cols = [grp(ker[t - s]) if t >= s else grp(zero) for t in range(T)]
        blk = jnp.stack(cols, axis=2)
        rows.append(jnp.einsum('nqtdc,qh->nqcthd', blk, eye_q))
    w_intra = jnp.stack(rows, axis=1).reshape(SSM_NQ, T * Q * C, T * Q * C)

    cols = []
    for t in range(T):
        e_re, e_im = _cmul(cr, ci, pw[t + 1][0][:, None, :], pw[t + 1][1][:, None, :])
        blocks = jnp.stack([grp(e_re), grp(-e_im)], axis=2)
        cols.append(jnp.einsum('nqrdp,qh->nrqphd', blocks, eye_q))
    w_inter = jnp.stack(cols, axis=4).reshape(SSM_NQ, 2 * Q * P, T * Q * C)

    a = pw[T]
    lev_re, lev_im = [], []
    level = 0
    while (1 << level) < n_chunks:
        lev_re.append(grp(a[0]).reshape(SSM_NQ, Q * P))
        lev_im.append(grp(a[1]).reshape(SSM_NQ, Q * P))
        a = _cmul(a[0], a[1], a[0], a[1])
        level += 1
    return (w_state.astype(BF16), w_intra.astype(BF16), w_inter.astype(BF16),
            jnp.stack(lev_re, axis=1), jnp.stack(lev_im, axis=1))


def _merge_kernel(oa_ref, oc_ref, os_ref, ga_ref, gc_ref, gs_ref, wa_ref, wc_ref, ws_ref, o_ref):
    ya = jnp.dot(oa_ref[...], wa_ref[...], preferred_element_type=F32)
    yc = jnp.dot(oc_ref[...], wc_ref[...], preferred_element_type=F32)
    ys = jnp.dot(os_ref[...], ws_ref[...], preferred_element_type=F32)
    merged = (jax.nn.sigmoid(ga_ref[...]) * ya + jax.nn.sigmoid(gc_ref[...]) * yc
              + jax.nn.sigmoid(gs_ref[...]) * ys)
    o_ref[...] = merged.astype(o_ref.dtype)


def branch_merge(o_attn, o_conv, o_ssm, proj, w_branch, bm=1024, bn=512):
    m = proj.shape[0]
    bm = min(bm, m)
    gcol = OFF_GATE // bn
    gstep = D_MODEL // bn
    blk = (_nbytes((bm, MIX_WIDTH), BF16) + 3 * _nbytes((bm, bn), F32)
           + _nbytes((MIX_WIDTH, bn), BF16) + _nbytes((bm, bn), BF16))
    return pl.pallas_call(
        _merge_kernel,
        grid=(m // bm, D_MODEL // bn),
        in_specs=[pl.BlockSpec((bm, ATTN_WIDTH), lambda i, j: (i, 0)),
                  pl.BlockSpec((bm, CONV_WIDTH), lambda i, j: (i, 0)),
                  pl.BlockSpec((bm, SSM_WIDTH), lambda i, j: (i, 0)),
                  pl.BlockSpec((bm, bn), lambda i, j: (i, gcol + j)),
                  pl.BlockSpec((bm, bn), lambda i, j: (i, gcol + gstep + j)),
                  pl.BlockSpec((bm, bn), lambda i, j: (i, gcol + 2 * gstep + j)),
                  pl.BlockSpec((ATTN_WIDTH, bn), lambda i, j: (0, j)),
                  pl.BlockSpec((CONV_WIDTH, bn), lambda i, j: (ATTN_WIDTH // CONV_WIDTH, j)),
                  pl.BlockSpec((SSM_WIDTH, bn), lambda i, j: ((ATTN_WIDTH + CONV_WIDTH) // SSM_WIDTH, j))],
        out_specs=pl.BlockSpec((bm, bn), lambda i, j: (i, j)),
        out_shape=jax.ShapeDtypeStruct((m, D_MODEL), BF16),
        compiler_params=_params(blk, semantics=("arbitrary", "arbitrary")),
        name="branch_merge",
    )(o_attn, o_conv, o_ssm, proj, proj, proj, w_branch, w_branch, w_branch)


def _ffn(x, norm_g, w_in, w_out):
    h = rmsnorm(x, norm_g)
    act = swiglu_in(h, w_in.astype(BF16))
    return matmul_res(act, w_out.astype(BF16), x, 0.5)


def _layer(x, seq, bias, p):
    x = _ffn(x, p['ffn1_norm'], p['w_ffn1_in'], p['w_ffn1_out'])
    h = rmsnorm(x, p['mix_norm'])
    proj = matmul(h, p['w_in'].astype(BF16), F32)
    o_attn = attention(proj, bias, p['q_norm'], p['k_norm'], p['attn_sinks'], seq)
    o_conv = conv_branch(proj, p['conv_dw'], p['conv_dw_bias'], p['conv_ln_g'], p['conv_ln_b'],
                         p['conv_pw'].astype(BF16), seq)
    ops = ssm_operands(p['ssm_a_re'], p['ssm_a_im'], p['ssm_log_dt'], p['ssm_b_re'], p['ssm_b_im'],
                       p['ssm_c_re'], p['ssm_c_im'], seq // SSM_CHUNK)
    y_ssm = ssm_branch(proj, *ops, p['ssm_d'], seq)
    o_ssm = glu_matmul(y_ssm, p['ssm_glu'].astype(BF16))
    merged = branch_merge(o_attn, o_conv, o_ssm, proj, p['w_branch'].astype(BF16))
    x = matmul_res(merged, p['w_out'].astype(BF16), x, 1.0)
    return _ffn(x, p['ffn2_norm'], p['w_ffn2_in'], p['w_ffn2_out'])


def kernel(x, rel_bias, ffn1_norm, w_ffn1_in, w_ffn1_out, mix_norm, w_in, q_norm, k_norm, attn_sinks,
           conv_dw, conv_dw_bias, conv_ln_g, conv_ln_b, conv_pw, ssm_a_re, ssm_a_im, ssm_log_dt,
           ssm_b_re, ssm_b_im, ssm_c_re, ssm_c_im, ssm_d, ssm_glu, w_branch, w_out, ffn2_norm,
           w_ffn2_in, w_ffn2_out):
    layer_params = dict(
        ffn1_norm=ffn1_norm, w_ffn1_in=w_ffn1_in, w_ffn1_out=w_ffn1_out, mix_norm=mix_norm, w_in=w_in,
        q_norm=q_norm, k_norm=k_norm, attn_sinks=attn_sinks, conv_dw=conv_dw, conv_dw_bias=conv_dw_bias,
        conv_ln_g=conv_ln_g, conv_ln_b=conv_ln_b, conv_pw=conv_pw, ssm_a_re=ssm_a_re, ssm_a_im=ssm_a_im,
        ssm_log_dt=ssm_log_dt, ssm_b_re=ssm_b_re, ssm_b_im=ssm_b_im, ssm_c_re=ssm_c_re, ssm_c_im=ssm_c_im,
        ssm_d=ssm_d, ssm_glu=ssm_glu, w_branch=w_branch, w_out=w_out, ffn2_norm=ffn2_norm,
        w_ffn2_in=w_ffn2_in, w_ffn2_out=w_ffn2_out)
    b, seq, d = x.shape
    bias = band_bias(rel_bias)
    xf = x.reshape(b * seq, d)
    for i in range(DEPTH):
        xf = _layer(xf, seq, bias, {k: v[i] for k, v in layer_params.items()})
    return xf.reshape(b, seq, d)
```

```python
import functools
import math

import jax
import jax.numpy as jnp
from jax import lax
from jax.experimental import pallas as pl
from jax.experimental.pallas import tpu as pltpu

F32 = jnp.float32
BF16 = jnp.bfloat16

D_MODEL = 4096
DEPTH = 2
N_Q_HEADS = 32
N_KV_HEADS = 8
HEAD_DIM = 64
Q_GROUP = N_Q_HEADS // N_KV_HEADS
ATTN_WIDTH = N_Q_HEADS * HEAD_DIM
KV_WIDTH = N_KV_HEADS * HEAD_DIM
WINDOW = 128
BLOCK = 128
N_BUCKETS = 32
MAX_DISTANCE = 128
CONV_WIDTH = D_MODEL // 4
CONV_KERNEL = 31
SSM_WIDTH = D_MODEL // 4
SSM_GROUP = 16
SSM_GROUPS = SSM_WIDTH // SSM_GROUP
SSM_STATE = 64
D_FF = 256 * ((8 * D_MODEL // 3 + 255) // 256)
N_BRANCH = 3
MIX_WIDTH = ATTN_WIDTH + CONV_WIDTH + SSM_WIDTH
IN_WIDTH = ATTN_WIDTH + 2 * KV_WIDTH + 2 * CONV_WIDTH + SSM_WIDTH + N_BRANCH * D_MODEL
EPS = 1e-6

OFF_K = ATTN_WIDTH
OFF_V = OFF_K + KV_WIDTH
OFF_CONV = OFF_V + KV_WIDTH
OFF_SSM = OFF_CONV + 2 * CONV_WIDTH
OFF_GATE = OFF_SSM + SSM_WIDTH

V7X_LANES = 128
V7X_VMEM_BYTES = 64 * 1024 * 1024
V7X_VMEM_INTERNAL_BYTES = 12 * 1024 * 1024

SSM_CHUNK = 8
SSM_QGROUPS = 8
SSM_QCH = SSM_QGROUPS * SSM_GROUP
SSM_NQ = SSM_GROUPS // SSM_QGROUPS
SSM_QSTATE = SSM_QGROUPS * SSM_STATE


def _nbytes(shape, dtype):
    return math.prod(shape) * jnp.dtype(dtype).itemsize


def _params(block_bytes, scratch_bytes=0, semantics=None, single_bytes=0):
    limit = 2 * block_bytes + single_bytes + scratch_bytes + V7X_VMEM_INTERNAL_BYTES
    limit = min(limit, V7X_VMEM_BYTES - 4 * 1024 * 1024)
    return pltpu.CompilerParams(dimension_semantics=semantics, vmem_limit_bytes=int(limit))


def _resident(shape, index_map):
    return pl.BlockSpec(shape, index_map, pipeline_mode=pl.Buffered(1))


def _weight(k, bn, layer, col):
    return pl.BlockSpec((None, k, bn), lambda i, j: (layer, 0, col(j)))


def _wdot(a, w_ref):
    return jnp.dot(a, w_ref[...].astype(BF16), preferred_element_type=F32)


def _rmsnorm_kernel(x_ref, g_ref, o_ref):
    x = x_ref[...]
    ms = jnp.mean(x * x, axis=-1, keepdims=True)
    o_ref[...] = (x * lax.rsqrt(ms + EPS) * g_ref[...]).astype(o_ref.dtype)


def rmsnorm(x, g, bm=256):
    m, d = x.shape
    blk = _nbytes((bm, d), F32) + _nbytes((bm, d), BF16) + _nbytes((1, d), F32)
    return pl.pallas_call(
        _rmsnorm_kernel,
        grid=(m // bm,),
        in_specs=[pl.BlockSpec((bm, d), lambda i: (i, 0)),
                  pl.BlockSpec((1, d), lambda i: (0, 0))],
        out_specs=pl.BlockSpec((bm, d), lambda i: (i, 0)),
        out_shape=jax.ShapeDtypeStruct((m, d), BF16),
        compiler_params=_params(blk, semantics=("arbitrary",)),
        name="rmsnorm",
    )(x, g.reshape(1, d))


def _swiglu_kernel(h_ref, wg_ref, wu_ref, o_ref):
    h = h_ref[...]
    g = _wdot(h, wg_ref)
    u = _wdot(h, wu_ref)
    o_ref[...] = (g * jax.nn.sigmoid(g) * u).astype(o_ref.dtype)


def swiglu_in(h, w, layer, bm=2048, bn=256):
    m, k = h.shape
    f = w.shape[2] // 2
    bm = min(bm, m)
    nj = f // bn
    blk = 2 * _nbytes((k, bn), F32) + _nbytes((bm, bn), BF16)
    tmp = 2 * _nbytes((k, bn), BF16) + 2 * _nbytes((bm, bn), F32)
    return pl.pallas_call(
        _swiglu_kernel,
        grid=(m // bm, nj),
        in_specs=[_resident((bm, k), lambda i, j: (i, 0)),
                  _weight(k, bn, layer, lambda j: j),
                  _weight(k, bn, layer, lambda j: j + nj)],
        out_specs=pl.BlockSpec((bm, bn), lambda i, j: (i, j)),
        out_shape=jax.ShapeDtypeStruct((m, f), BF16),
        compiler_params=_params(blk, tmp, ("arbitrary", "arbitrary"), _nbytes((bm, k), BF16)),
        name="swiglu_in",
    )(h, w, w)


def _matmul_kernel(a_ref, w_ref, o_ref):
    o_ref[...] = _wdot(a_ref[...], w_ref).astype(o_ref.dtype)


def matmul(a, w, layer, out_dtype, bm=2048, bn=512):
    m, k = a.shape
    n = w.shape[2]
    bm, bn = min(bm, m), min(bn, n)
    blk = _nbytes((k, bn), F32) + _nbytes((bm, bn), out_dtype)
    tmp = _nbytes((k, bn), BF16) + _nbytes((bm, bn), F32)
    return pl.pallas_call(
        _matmul_kernel,
        grid=(m // bm, n // bn),
        in_specs=[_resident((bm, k), lambda i, j: (i, 0)),
                  _weight(k, bn, layer, lambda j: j)],
        out_specs=pl.BlockSpec((bm, bn), lambda i, j: (i, j)),
        out_shape=jax.ShapeDtypeStruct((m, n), out_dtype),
        compiler_params=_params(blk, tmp, ("arbitrary", "arbitrary"), _nbytes((bm, k), a.dtype)),
        name="matmul",
    )(a, w)


def _matmul_res_kernel(a_ref, w_ref, x_ref, o_ref, *, scale):
    o_ref[...] = x_ref[...] + scale * _wdot(a_ref[...], w_ref)


def matmul_res(a, w, layer, x, scale, bm, bn=256):
    m, k = a.shape
    n = w.shape[2]
    bm, bn = min(bm, m), min(bn, n)
    blk = _nbytes((k, bn), F32) + 2 * _nbytes((bm, bn), F32)
    tmp = _nbytes((k, bn), BF16) + _nbytes((bm, bn), F32)
    return pl.pallas_call(
        functools.partial(_matmul_res_kernel, scale=scale),
        grid=(m // bm, n // bn),
        in_specs=[_resident((bm, k), lambda i, j: (i, 0)),
                  _weight(k, bn, layer, lambda j: j),
                  pl.BlockSpec((bm, bn), lambda i, j: (i, j))],
        out_specs=pl.BlockSpec((bm, bn), lambda i, j: (i, j)),
        out_shape=jax.ShapeDtypeStruct((m, n), F32),
        compiler_params=_params(blk, tmp, ("arbitrary", "arbitrary"), _nbytes((bm, k), a.dtype)),
        name="matmul_res",
    )(a, w, x)


def _glu_kernel(a_ref, wa_ref, wb_ref, o_ref):
    a = a_ref[...].astype(BF16)
    o_ref[...] = (_wdot(a, wa_ref) * jax.nn.sigmoid(_wdot(a, wb_ref))).astype(o_ref.dtype)


def glu_matmul(a, w, layer, bm=2048, bn=256):
    m, k = a.shape
    n = w.shape[2] // 2
    bm = min(bm, m)
    nj = n // bn
    blk = 2 * _nbytes((k, bn), F32) + _nbytes((bm, bn), BF16)
    tmp = _nbytes((bm, k), BF16) + 2 * _nbytes((k, bn), BF16) + 2 * _nbytes((bm, bn), F32)
    return pl.pallas_call(
        _glu_kernel,
        grid=(m // bm, nj),
        in_specs=[_resident((bm, k), lambda i, j: (i, 0)),
                  _weight(k, bn, layer, lambda j: j),
                  _weight(k, bn, layer, lambda j: j + nj)],
        out_specs=pl.BlockSpec((bm, bn), lambda i, j: (i, j)),
        out_shape=jax.ShapeDtypeStruct((m, n), BF16),
        compiler_params=_params(blk, tmp, ("arbitrary", "arbitrary"), _nbytes((bm, k), a.dtype)),
        name="glu_matmul",
    )(a, w, w)


def _segment_mean_sq(x, seg_ref):
    s = jnp.dot(x * x, seg_ref[...], preferred_element_type=F32, precision=lax.Precision.HIGHEST)
    return s * (1.0 / HEAD_DIM)


def _attn_kernel(sink_ref, q_ref, kp_ref, kc_ref, vp_ref, vc_ref, bias_ref, qg_ref, kg_ref,
                 seg_ref, o_ref, *, nb):
    n = pl.program_id(0) % nb
    lanes = V7X_LANES

    kk = jnp.concatenate([kp_ref[...], kc_ref[...]], axis=0)
    vv = jnp.concatenate([vp_ref[...], vc_ref[...]], axis=0)
    ks = []
    for c in range(KV_WIDTH // lanes):
        kc = kk[:, c * lanes:(c + 1) * lanes]
        ks.append(kc * lax.rsqrt(_segment_mean_sq(kc, seg_ref) + EPS))
    kn = jnp.concatenate(ks, axis=1) * kg_ref[...]

    r = lax.broadcasted_iota(jnp.int32, (BLOCK, 2 * BLOCK), 0)
    c = lax.broadcasted_iota(jnp.int32, (BLOCK, 2 * BLOCK), 1)
    dist = BLOCK + r - c
    mask = (dist >= 0) & (dist < WINDOW) & ((n > 0) | (c >= BLOCK))
    low = lax.broadcasted_iota(jnp.int32, (1, lanes), 1) < HEAD_DIM

    for pair in range(ATTN_WIDTH // lanes):
        kv = pair // (Q_GROUP // 2)
        kvp, odd = kv // 2, kv % 2
        kpair = kn[:, kvp * lanes:(kvp + 1) * lanes]
        vpair = vv[:, kvp * lanes:(kvp + 1) * lanes]
        kswap = pltpu.roll(kpair, HEAD_DIM, axis=1)
        vswap = pltpu.roll(vpair, HEAD_DIM, axis=1)
        kdup = (jnp.where(low, kswap, kpair) if odd else jnp.where(low, kpair, kswap)).astype(BF16)
        vdup = (jnp.where(low, vswap, vpair) if odd else jnp.where(low, vpair, vswap)).astype(BF16)

        q2 = q_ref[:, pair * lanes:(pair + 1) * lanes]
        q2 = q2 * lax.rsqrt(_segment_mean_sq(q2, seg_ref) + EPS)
        q2 = q2 * qg_ref[:, pair * lanes:(pair + 1) * lanes] * (HEAD_DIM ** -0.5)
        outs = []
        for half in range(2):
            head = 2 * pair + half
            sel = low if half == 0 else jnp.logical_not(low)
            qh = jnp.where(sel, q2, 0.0).astype(BF16)
            s = lax.dot_general(qh, kdup, (((1,), (1,)), ((), ())), preferred_element_type=F32)
            s = s + bias_ref[head]
            s = jnp.where(mask, s, -jnp.inf)
            sink = sink_ref[head]
            mx = jnp.maximum(jnp.max(s, axis=-1, keepdims=True), sink)
            p = jnp.exp(s - mx)
            den = jnp.sum(p, axis=-1, keepdims=True) + jnp.exp(sink - mx)
            o = jnp.dot(p.astype(BF16), vdup, preferred_element_type=F32)
            outs.append(o / den)
        o_ref[:, pair * lanes:(pair + 1) * lanes] = jnp.where(low, outs[0], outs[1]).astype(o_ref.dtype)


def attention(proj, bias, q_gain, k_gain, sinks, seq):
    m = proj.shape[0]
    nb = seq // BLOCK
    nblk = m // BLOCK
    lanes = V7X_LANES
    seg_id = jnp.arange(lanes) // HEAD_DIM
    seg = (seg_id[:, None] == seg_id[None, :]).astype(F32)
    qg = jnp.tile(q_gain.astype(F32), N_Q_HEADS).reshape(1, ATTN_WIDTH)
    kg = jnp.tile(k_gain.astype(F32), N_KV_HEADS).reshape(1, KV_WIDTH)
    kcol, vcol = OFF_K // KV_WIDTH, OFF_V // KV_WIDTH

    def prev(i):
        return jnp.where(i % nb == 0, i, i - 1)

    blk = (_nbytes((BLOCK, ATTN_WIDTH), F32) + 4 * _nbytes((BLOCK, KV_WIDTH), F32)
           + _nbytes(bias.shape, F32) + _nbytes((BLOCK, ATTN_WIDTH), BF16))
    return pl.pallas_call(
        functools.partial(_attn_kernel, nb=nb),
        grid=(nblk,),
        in_specs=[pl.BlockSpec(memory_space=pltpu.SMEM),
                  pl.BlockSpec((BLOCK, ATTN_WIDTH), lambda i: (i, 0)),
                  pl.BlockSpec((BLOCK, KV_WIDTH), lambda i: (prev(i), kcol)),
                  pl.BlockSpec((BLOCK, KV_WIDTH), lambda i: (i, kcol)),
                  pl.BlockSpec((BLOCK, KV_WIDTH), lambda i: (prev(i), vcol)),
                  pl.BlockSpec((BLOCK, KV_WIDTH), lambda i: (i, vcol)),
                  pl.BlockSpec(bias.shape, lambda i: (0, 0, 0)),
                  pl.BlockSpec((1, ATTN_WIDTH), lambda i: (0, 0)),
                  pl.BlockSpec((1, KV_WIDTH), lambda i: (0, 0)),
                  pl.BlockSpec((lanes, lanes), lambda i: (0, 0))],
        out_specs=pl.BlockSpec((BLOCK, ATTN_WIDTH), lambda i: (i, 0)),
        out_shape=jax.ShapeDtypeStruct((m, ATTN_WIDTH), BF16),
        compiler_params=_params(blk, semantics=("arbitrary",)),
        name="swa_attention",
    )(sinks.astype(F32), proj, proj, proj, proj, proj, bias, qg, kg, seg)


def _t5_bucket(dist):
    max_exact = N_BUCKETS // 2
    n = jnp.maximum(dist, 0)
    ratio = jnp.log(jnp.maximum(n, 1).astype(F32) / max_exact) / math.log(MAX_DISTANCE / max_exact)
    large = jnp.minimum(max_exact + (ratio * (N_BUCKETS - max_exact)).astype(jnp.int32), N_BUCKETS - 1)
    return jnp.where(n < max_exact, n, large)


def band_bias(rel_bias):
    r = jnp.arange(BLOCK)[:, None]
    c = jnp.arange(2 * BLOCK)[None, :]
    bias = rel_bias.astype(F32)[_t5_bucket(BLOCK + r - c)]
    return jnp.transpose(bias, (2, 0, 1))


CONV_HALO = 32
CONV_ROWS = 32


def _conv_kernel(a_ref, g_ref, w_ref, b_ref, lg_ref, lb_ref, pw_ref, o_ref, ubuf, ybuf, *, nt, tt):
    t = pl.program_id(0) % nt

    @pl.when(t == 0)
    def _():
        ubuf[0:CONV_HALO, :] = jnp.zeros((CONV_HALO, CONV_WIDTH), F32)

    @pl.when(t != 0)
    def _():
        ubuf[0:CONV_HALO, :] = ubuf[tt:tt + CONV_HALO, :]

    ubuf[CONV_HALO:CONV_HALO + tt, :] = a_ref[...] * jax.nn.sigmoid(g_ref[...])

    first = CONV_HALO - (CONV_KERNEL - 1)
    cw = 2 * V7X_LANES

    def rows(rb, carry):
        base = pl.multiple_of(rb * CONV_ROWS, CONV_ROWS)
        for cb in range(CONV_WIDTH // cw):
            win = ubuf[pl.ds(base, CONV_ROWS + CONV_HALO), cb * cw:(cb + 1) * cw]
            acc = jnp.zeros((CONV_ROWS, cw), F32) + b_ref[:, cb * cw:(cb + 1) * cw]
            for j in range(CONV_KERNEL):
                acc = acc + w_ref[j:j + 1, cb * cw:(cb + 1) * cw] * win[first + j:first + j + CONV_ROWS, :]
            ybuf[pl.ds(base, CONV_ROWS), cb * cw:(cb + 1) * cw] = acc
        return carry

    lax.fori_loop(0, tt // CONV_ROWS, rows, 0)

    y = ybuf[...]
    mu = jnp.mean(y, axis=-1, keepdims=True)
    yc = y - mu
    var = jnp.mean(yc * yc, axis=-1, keepdims=True)
    z = yc * lax.rsqrt(var + EPS) * lg_ref[...] + lb_ref[...]
    z = z * jax.nn.sigmoid(z)
    o_ref[...] = jnp.dot(z.astype(BF16), pw_ref[...], preferred_element_type=F32).astype(o_ref.dtype)


def conv_branch(proj, w_dw, b_dw, ln_g, ln_b, w_pw, seq, tt=256):
    m = proj.shape[0]
    tt = min(tt, seq)
    nt = seq // tt
    acol = OFF_CONV // CONV_WIDTH
    w_pad = jnp.zeros((CONV_HALO, CONV_WIDTH), F32).at[:CONV_KERNEL].set(w_dw.astype(F32))
    row = lambda v: v.astype(F32).reshape(1, CONV_WIDTH)
    blk = (2 * _nbytes((tt, CONV_WIDTH), F32) + _nbytes((CONV_HALO, CONV_WIDTH), F32)
           + 3 * _nbytes((1, CONV_WIDTH), F32) + _nbytes((CONV_WIDTH, CONV_WIDTH), BF16)
           + _nbytes((tt, CONV_WIDTH), BF16))
    scratch = _nbytes((tt + CONV_HALO, CONV_WIDTH), F32) + _nbytes((tt, CONV_WIDTH), F32)
    return pl.pallas_call(
        functools.partial(_conv_kernel, nt=nt, tt=tt),
        grid=(m // tt,),
        in_specs=[pl.BlockSpec((tt, CONV_WIDTH), lambda i: (i, acol)),
                  pl.BlockSpec((tt, CONV_WIDTH), lambda i: (i, acol + 1)),
                  pl.BlockSpec((CONV_HALO, CONV_WIDTH), lambda i: (0, 0)),
                  pl.BlockSpec((1, CONV_WIDTH), lambda i: (0, 0)),
                  pl.BlockSpec((1, CONV_WIDTH), lambda i: (0, 0)),
                  pl.BlockSpec((1, CONV_WIDTH), lambda i: (0, 0)),
                  pl.BlockSpec((CONV_WIDTH, CONV_WIDTH), lambda i: (0, 0))],
        out_specs=pl.BlockSpec((tt, CONV_WIDTH), lambda i: (i, 0)),
        out_shape=jax.ShapeDtypeStruct((m, CONV_WIDTH), BF16),
        scratch_shapes=[pltpu.VMEM((tt + CONV_HALO, CONV_WIDTH), F32),
                        pltpu.VMEM((tt, CONV_WIDTH), F32)],
        compiler_params=_params(blk, scratch, semantics=("arbitrary",)),
        name="conv_branch",
    )(proj, proj, w_pad, row(b_dw), row(ln_g), row(ln_b), w_pw)


def _ssm_kernel(u_ref, ws_ref, wi_ref, wx_ref, are_ref, aim_ref, d_ref, o_ref, *, nc):
    x = jnp.concatenate([u_ref[pl.ds(t, nc, stride=SSM_CHUNK), :] for t in range(SSM_CHUNK)], axis=1)
    xb = x.astype(BF16)
    s = jnp.dot(xb, ws_ref[0], preferred_element_type=F32)
    h_re, h_im = s[:, :SSM_QSTATE], s[:, SSM_QSTATE:]
    row = lax.broadcasted_iota(jnp.int32, (nc, 1), 0)
    level = 0
    while (1 << level) < nc:
        d = 1 << level
        keep = row >= d
        p_re = jnp.where(keep, pltpu.roll(h_re, d, axis=0), 0.0)
        p_im = jnp.where(keep, pltpu.roll(h_im, d, axis=0), 0.0)
        a_re = are_ref[0, level:level + 1, :]
        a_im = aim_ref[0, level:level + 1, :]
        h_re, h_im = (h_re + a_re * p_re - a_im * p_im,
                      h_im + a_re * p_im + a_im * p_re)
        level += 1
    keep = row >= 1
    hp = jnp.concatenate([jnp.where(keep, pltpu.roll(h_re, 1, axis=0), 0.0),
                          jnp.where(keep, pltpu.roll(h_im, 1, axis=0), 0.0)], axis=1)
    y = (jnp.dot(xb, wi_ref[0], preferred_element_type=F32)
         + jnp.dot(hp.astype(BF16), wx_ref[0], preferred_element_type=F32)
         + d_ref[0] * x)
    y = jax.nn.gelu(y)
    for t in range(SSM_CHUNK):
        o_ref[pl.ds(t, nc, stride=SSM_CHUNK), :] = y[:, t * SSM_QCH:(t + 1) * SSM_QCH]


def ssm_branch(proj, w_state, w_intra, w_inter, a_re, a_im, d_skip, seq):
    m = proj.shape[0]
    nseq = m // seq
    nc = seq // SSM_CHUNK
    ucol = OFF_SSM // SSM_QCH
    kw = SSM_CHUNK * SSM_QCH
    nlev = a_re.shape[1]
    d_t = jnp.tile(d_skip.astype(F32).reshape(SSM_NQ, 1, SSM_QCH), (1, 1, SSM_CHUNK))
    blk = (2 * _nbytes((seq, SSM_QCH), F32) + 3 * _nbytes((kw, kw), BF16)
           + 2 * _nbytes((nlev, SSM_QSTATE), F32) + _nbytes((1, kw), F32))
    return pl.pallas_call(
        functools.partial(_ssm_kernel, nc=nc),
        grid=(SSM_NQ, nseq),
        in_specs=[pl.BlockSpec((seq, SSM_QCH), lambda q, b: (b, ucol + q)),
                  pl.BlockSpec((1, kw, 2 * SSM_QSTATE), lambda q, b: (q, 0, 0)),
                  pl.BlockSpec((1, kw, kw), lambda q, b: (q, 0, 0)),
                  pl.BlockSpec((1, 2 * SSM_QSTATE, kw), lambda q, b: (q, 0, 0)),
                  pl.BlockSpec((1, nlev, SSM_QSTATE), lambda q, b: (q, 0, 0)),
                  pl.BlockSpec((1, nlev, SSM_QSTATE), lambda q, b: (q, 0, 0)),
                  pl.BlockSpec((1, 1, kw), lambda q, b: (q, 0, 0))],
        out_specs=pl.BlockSpec((seq, SSM_QCH), lambda q, b: (b, q)),
        out_shape=jax.ShapeDtypeStruct((m, SSM_WIDTH), F32),
        compiler_params=_params(blk, semantics=("arbitrary", "arbitrary")),
        name="ssm_scan",
    )(proj, w_state, w_intra, w_inter, a_re, a_im, d_t)


def _cmul(ar, ai, br, bi):
    return ar * br - ai * bi, ar * bi + ai * br


def ssm_operands(a_re, a_im, log_dt, b_re, b_im, c_re, c_im, n_chunks):
    f32 = F32
    dt = jnp.exp(log_dt.astype(f32))[:, None]
    lam_re, lam_im = a_re.astype(f32), a_im.astype(f32)
    mag = jnp.exp(dt * lam_re)
    ang = dt * lam_im
    lb_re, lb_im = mag * jnp.cos(ang), mag * jnp.sin(ang)
    nr = lb_re - 1.0
    den = lam_re * lam_re + lam_im * lam_im
    coef_re = (nr * lam_re + lb_im * lam_im) / den
    coef_im = (lb_im * lam_re - nr * lam_im) / den
    br, bi = b_re.astype(f32), b_im.astype(f32)
    bb_re = coef_re[..., None] * br - coef_im[..., None] * bi
    bb_im = coef_re[..., None] * bi + coef_im[..., None] * br
    cr, ci = c_re.astype(f32), c_im.astype(f32)

    pw = [(jnp.ones_like(lb_re), jnp.zeros_like(lb_im))]
    for _ in range(SSM_CHUNK):
        pw.append(_cmul(pw[-1][0], pw[-1][1], lb_re, lb_im))

    T, Q, C, P = SSM_CHUNK, SSM_QGROUPS, SSM_GROUP, SSM_STATE
    eye_q = jnp.eye(Q, dtype=f32)
    grp = lambda v: v.reshape((SSM_NQ, Q) + v.shape[1:])

    ws = []
    for s in range(T):
        e_re, e_im = _cmul(pw[T - 1 - s][0][..., None], pw[T - 1 - s][1][..., None], bb_re, bb_im)
        blocks = jnp.stack([grp(e_re), grp(e_im)], axis=2)
        ws.append(jnp.einsum('nqrpc,qh->nqcrhp', blocks, eye_q))
    w_state = jnp.stack(ws, axis=1).reshape(SSM_NQ, T * Q * C, 2 * Q * P)

    ker = []
    for l in range(T):
        e_re, e_im = _cmul(pw[l][0][..., None], pw[l][1][..., None], bb_re, bb_im)
        ker.append(jnp.einsum('gdp,gpc->gdc', cr, e_re, precision=lax.Precision.HIGHEST)
                   - jnp.einsum('gdp,gpc->gdc', ci, e_im, precision=lax.Precision.HIGHEST))
    zero = jnp.zeros_like(ker[0])
    rows = []
    for s in range(T):
        cols = [grp(ker[t - s]) if t >= s else grp(zero) for t in range(T)]
        blk = jnp.stack(cols, axis=2)
        rows.append(jnp.einsum('nqtdc,qh->nqcthd', blk, eye_q))
    w_intra = jnp.stack(rows, axis=1).reshape(SSM_NQ, T * Q * C, T * Q * C)

    cols = []
    for t in range(T):
        e_re, e_im = _cmul(cr, ci, pw[t + 1][0][:, None, :], pw[t + 1][1][:, None, :])
        blocks = jnp.stack([grp(e_re), grp(-e_im)], axis=2)
        cols.append(jnp.einsum('nqrdp,qh->nrqphd', blocks, eye_q))
    w_inter = jnp.stack(cols, axis=4).reshape(SSM_NQ, 2 * Q * P, T * Q * C)

    a = pw[T]
    lev_re, lev_im = [], []
    level = 0
    while (1 << level) < n_chunks:
        lev_re.append(grp(a[0]).reshape(SSM_NQ, Q * P))
        lev_im.append(grp(a[1]).reshape(SSM_NQ, Q * P))
        a = _cmul(a[0], a[1], a[0], a[1])
        level += 1
    return (w_state.astype(BF16), w_intra.astype(BF16), w_inter.astype(BF16),
            jnp.stack(lev_re, axis=1), jnp.stack(lev_im, axis=1))


def _merge_kernel(oa_ref, oc_ref, os_ref, ga_ref, gc_ref, gs_ref, wa_ref, wc_ref, ws_ref, o_ref):
    ya = _wdot(oa_ref[...], wa_ref)
    yc = _wdot(oc_ref[...], wc_ref)
    ys = _wdot(os_ref[...], ws_ref)
    merged = (jax.nn.sigmoid(ga_ref[...]) * ya + jax.nn.sigmoid(gc_ref[...]) * yc
              + jax.nn.sigmoid(gs_ref[...]) * ys)
    o_ref[...] = merged.astype(o_ref.dtype)


def branch_merge(o_attn, o_conv, o_ssm, proj, w_branch, layer, bm=2048, bn=256):
    m = proj.shape[0]
    bm = min(bm, m)
    gcol = OFF_GATE // bn
    gstep = D_MODEL // bn
    blk = 3 * _nbytes((bm, bn), F32) + _nbytes((MIX_WIDTH, bn), F32) + _nbytes((bm, bn), BF16)
    tmp = _nbytes((MIX_WIDTH, bn), BF16) + 4 * _nbytes((bm, bn), F32)

    def wrows(rows, row_tile):
        return pl.BlockSpec((None, rows, bn), lambda i, j: (layer, row_tile, j))

    return pl.pallas_call(
        _merge_kernel,
        grid=(m // bm, D_MODEL // bn),
        in_specs=[_resident((bm, ATTN_WIDTH), lambda i, j: (i, 0)),
                  _resident((bm, CONV_WIDTH), lambda i, j: (i, 0)),
                  _resident((bm, SSM_WIDTH), lambda i, j: (i, 0)),
                  pl.BlockSpec((bm, bn), lambda i, j: (i, gcol + j)),
                  pl.BlockSpec((bm, bn), lambda i, j: (i, gcol + gstep + j)),
                  pl.BlockSpec((bm, bn), lambda i, j: (i, gcol + 2 * gstep + j)),
                  wrows(ATTN_WIDTH, 0),
                  wrows(CONV_WIDTH, ATTN_WIDTH // CONV_WIDTH),
                  wrows(SSM_WIDTH, (ATTN_WIDTH + CONV_WIDTH) // SSM_WIDTH)],
        out_specs=pl.BlockSpec((bm, bn), lambda i, j: (i, j)),
        out_shape=jax.ShapeDtypeStruct((m, D_MODEL), BF16),
        compiler_params=_params(blk, tmp, ("arbitrary", "arbitrary"), _nbytes((bm, MIX_WIDTH), BF16)),
        name="branch_merge",
    )(o_attn, o_conv, o_ssm, proj, proj, proj, w_branch, w_branch, w_branch)


FFN_OUT_ROWS = 1024


def _ffn(x, norm_g, w_in, w_out, layer):
    h = rmsnorm(x, norm_g[layer])
    act = swiglu_in(h, w_in, layer)
    return matmul_res(act, w_out, layer, x, 0.5, bm=FFN_OUT_ROWS)


def _layer(x, seq, bias, p, layer):
    at = lambda name: p[name][layer]
    x = _ffn(x, p['ffn1_norm'], p['w_ffn1_in'], p['w_ffn1_out'], layer)
    h = rmsnorm(x, at('mix_norm'))
    proj = matmul(h, p['w_in'], layer, F32)
    o_attn = attention(proj, bias, at('q_norm'), at('k_norm'), at('attn_sinks'), seq)
    o_conv = conv_branch(proj, at('conv_dw'), at('conv_dw_bias'), at('conv_ln_g'), at('conv_ln_b'),
                         at('conv_pw').astype(BF16), seq)
    ops = ssm_operands(at('ssm_a_re'), at('ssm_a_im'), at('ssm_log_dt'), at('ssm_b_re'), at('ssm_b_im'),
                       at('ssm_c_re'), at('ssm_c_im'), seq // SSM_CHUNK)
    y_ssm = ssm_branch(proj, *ops, at('ssm_d'), seq)
    o_ssm = glu_matmul(y_ssm, p['ssm_glu'], layer)
    merged = branch_merge(o_attn, o_conv, o_ssm, proj, p['w_branch'], layer)
    x = matmul_res(merged, p['w_out'], layer, x, 1.0, bm=2048)
    return _ffn(x, p['ffn2_norm'], p['w_ffn2_in'], p['w_ffn2_out'], layer)


def kernel(x, rel_bias, ffn1_norm, w_ffn1_in, w_ffn1_out, mix_norm, w_in, q_norm, k_norm, attn_sinks,
           conv_dw, conv_dw_bias, conv_ln_g, conv_ln_b, conv_pw, ssm_a_re, ssm_a_im, ssm_log_dt,
           ssm_b_re, ssm_b_im, ssm_c_re, ssm_c_im, ssm_d, ssm_glu, w_branch, w_out, ffn2_norm,
           w_ffn2_in, w_ffn2_out):
    layer_params = dict(
        ffn1_norm=ffn1_norm, w_ffn1_in=w_ffn1_in, w_ffn1_out=w_ffn1_out, mix_norm=mix_norm, w_in=w_in,
        q_norm=q_norm, k_norm=k_norm, attn_sinks=attn_sinks, conv_dw=conv_dw, conv_dw_bias=conv_dw_bias,
        conv_ln_g=conv_ln_g, conv_ln_b=conv_ln_b, conv_pw=conv_pw, ssm_a_re=ssm_a_re, ssm_a_im=ssm_a_im,
        ssm_log_dt=ssm_log_dt, ssm_b_re=ssm_b_re, ssm_b_im=ssm_b_im, ssm_c_re=ssm_c_re, ssm_c_im=ssm_c_im,
        ssm_d=ssm_d, ssm_glu=ssm_glu, w_branch=w_branch, w_out=w_out, ffn2_norm=ffn2_norm,
        w_ffn2_in=w_ffn2_in, w_ffn2_out=w_ffn2_out)
    b, seq, d = x.shape
    bias = band_bias(rel_bias)
    xf = x.reshape(b * seq, d)
    for layer in range(DEPTH):
        xf = _layer(xf, seq, bias, layer_params, layer)
    return xf.reshape(b, seq, d)
```

```python
import functools
import math

import jax
import jax.numpy as jnp
from jax import lax
from jax.experimental import pallas as pl
from jax.experimental.pallas import tpu as pltpu

F32 = jnp.float32
BF16 = jnp.bfloat16

D_MODEL = 4096
DEPTH = 2
N_Q_HEADS = 32
N_KV_HEADS = 8
HEAD_DIM = 64
Q_GROUP = N_Q_HEADS // N_KV_HEADS
ATTN_WIDTH = N_Q_HEADS * HEAD_DIM
KV_WIDTH = N_KV_HEADS * HEAD_DIM
WINDOW = 128
BLOCK = 128
N_BUCKETS = 32
MAX_DISTANCE = 128
CONV_WIDTH = D_MODEL // 4
CONV_KERNEL = 31
SSM_WIDTH = D_MODEL // 4
SSM_GROUP = 16
SSM_GROUPS = SSM_WIDTH // SSM_GROUP
SSM_STATE = 64
D_FF = 256 * ((8 * D_MODEL // 3 + 255) // 256)
N_BRANCH = 3
MIX_WIDTH = ATTN_WIDTH + CONV_WIDTH + SSM_WIDTH
IN_WIDTH = ATTN_WIDTH + 2 * KV_WIDTH + 2 * CONV_WIDTH + SSM_WIDTH + N_BRANCH * D_MODEL
EPS = 1e-6

OFF_K = ATTN_WIDTH
OFF_V = OFF_K + KV_WIDTH
OFF_CONV = OFF_V + KV_WIDTH
OFF_SSM = OFF_CONV + 2 * CONV_WIDTH
OFF_GATE = OFF_SSM + SSM_WIDTH

V7X_LANES = 128
V7X_SUBLANES = 8
V7X_VMEM_BYTES = 64 * 1024 * 1024
V7X_VMEM_INTERNAL_BYTES = 12 * 1024 * 1024

SSM_CHUNK = 8
SSM_QGROUPS = 8
SSM_QCH = SSM_QGROUPS * SSM_GROUP
SSM_NQ = SSM_GROUPS // SSM_QGROUPS
SSM_QSTATE = SSM_QGROUPS * SSM_STATE


def _nbytes(shape, dtype):
    return math.prod(shape) * jnp.dtype(dtype).itemsize


def _params(block_bytes, scratch_bytes=0, semantics=None, single_bytes=0):
    limit = 2 * block_bytes + single_bytes + scratch_bytes + V7X_VMEM_INTERNAL_BYTES
    limit = min(limit, V7X_VMEM_BYTES - 4 * 1024 * 1024)
    return pltpu.CompilerParams(dimension_semantics=semantics, vmem_limit_bytes=int(limit))


def _resident(shape, index_map):
    return pl.BlockSpec(shape, index_map, pipeline_mode=pl.Buffered(1))


def _weight(k, bn, layer, col):
    return pl.BlockSpec((None, k, bn), lambda i, j: (layer, 0, col(j)))


def _wdot(a, w_ref):
    return jnp.dot(a, w_ref[...].astype(BF16), preferred_element_type=F32)


def _rmsnorm_kernel(x_ref, g_ref, o_ref):
    x = x_ref[...]
    ms = jnp.mean(x * x, axis=-1, keepdims=True)
    o_ref[...] = (x * lax.rsqrt(ms + EPS) * g_ref[...]).astype(o_ref.dtype)


def rmsnorm(x, g, bm=256):
    m, d = x.shape
    blk = _nbytes((bm, d), F32) + _nbytes((bm, d), BF16) + _nbytes((1, d), F32)
    return pl.pallas_call(
        _rmsnorm_kernel,
        grid=(m // bm,),
        in_specs=[pl.BlockSpec((bm, d), lambda i: (i, 0)),
                  pl.BlockSpec((1, d), lambda i: (0, 0))],
        out_specs=pl.BlockSpec((bm, d), lambda i: (i, 0)),
        out_shape=jax.ShapeDtypeStruct((m, d), BF16),
        compiler_params=_params(blk, semantics=("arbitrary",)),
        name="rmsnorm",
    )(x, g.reshape(1, d))


def _swiglu_kernel(h_ref, wg_ref, wu_ref, o_ref):
    h = h_ref[...]
    g = _wdot(h, wg_ref)
    u = _wdot(h, wu_ref)
    o_ref[...] = (g * jax.nn.sigmoid(g) * u).astype(o_ref.dtype)


def swiglu_in(h, w, layer, bm=2048, bn=256):
    m, k = h.shape
    f = w.shape[2] // 2
    bm = min(bm, m)
    nj = f // bn
    blk = 2 * _nbytes((k, bn), F32) + _nbytes((bm, bn), BF16)
    tmp = 2 * _nbytes((k, bn), BF16) + 2 * _nbytes((bm, bn), F32)
    return pl.pallas_call(
        _swiglu_kernel,
        grid=(m // bm, nj),
        in_specs=[_resident((bm, k), lambda i, j: (i, 0)),
                  _weight(k, bn, layer, lambda j: j),
                  _weight(k, bn, layer, lambda j: j + nj)],
        out_specs=pl.BlockSpec((bm, bn), lambda i, j: (i, j)),
        out_shape=jax.ShapeDtypeStruct((m, f), BF16),
        compiler_params=_params(blk, tmp, ("arbitrary", "arbitrary"), _nbytes((bm, k), BF16)),
        name="swiglu_in",
    )(h, w, w)


def _matmul_kernel(a_ref, w_ref, o_ref):
    o_ref[...] = _wdot(a_ref[...], w_ref).astype(o_ref.dtype)


def matmul(a, w, layer, out_dtype, bm=2048, bn=512):
    m, k = a.shape
    n = w.shape[2]
    bm, bn = min(bm, m), min(bn, n)
    blk = _nbytes((k, bn), F32) + _nbytes((bm, bn), out_dtype)
    tmp = _nbytes((k, bn), BF16) + _nbytes((bm, bn), F32)
    return pl.pallas_call(
        _matmul_kernel,
        grid=(m // bm, n // bn),
        in_specs=[_resident((bm, k), lambda i, j: (i, 0)),
                  _weight(k, bn, layer, lambda j: j)],
        out_specs=pl.BlockSpec((bm, bn), lambda i, j: (i, j)),
        out_shape=jax.ShapeDtypeStruct((m, n), out_dtype),
        compiler_params=_params(blk, tmp, ("arbitrary", "arbitrary"), _nbytes((bm, k), a.dtype)),
        name="matmul",
    )(a, w)


def _matmul_res_kernel(a_ref, w_ref, x_ref, o_ref, *, scale):
    o_ref[...] = x_ref[...] + scale * _wdot(a_ref[...], w_ref)


def matmul_res(a, w, layer, x, scale, bm, bn=256):
    m, k = a.shape
    n = w.shape[2]
    bm, bn = min(bm, m), min(bn, n)
    blk = _nbytes((k, bn), F32) + 2 * _nbytes((bm, bn), F32)
    tmp = _nbytes((k, bn), BF16) + _nbytes((bm, bn), F32)
    return pl.pallas_call(
        functools.partial(_matmul_res_kernel, scale=scale),
        grid=(m // bm, n // bn),
        in_specs=[_resident((bm, k), lambda i, j: (i, 0)),
                  _weight(k, bn, layer, lambda j: j),
                  pl.BlockSpec((bm, bn), lambda i, j: (i, j))],
        out_specs=pl.BlockSpec((bm, bn), lambda i, j: (i, j)),
        out_shape=jax.ShapeDtypeStruct((m, n), F32),
        compiler_params=_params(blk, tmp, ("arbitrary", "arbitrary"), _nbytes((bm, k), a.dtype)),
        name="matmul_res",
    )(a, w, x)


def _glu_kernel(a_ref, wa_ref, wb_ref, o_ref):
    a = a_ref[...].astype(BF16)
    o_ref[...] = (_wdot(a, wa_ref) * jax.nn.sigmoid(_wdot(a, wb_ref))).astype(o_ref.dtype)


def glu_matmul(a, w, layer, bm=2048, bn=256):
    m, k = a.shape
    n = w.shape[2] // 2
    bm = min(bm, m)
    nj = n // bn
    blk = 2 * _nbytes((k, bn), F32) + _nbytes((bm, bn), BF16)
    tmp = _nbytes((bm, k), BF16) + 2 * _nbytes((k, bn), BF16) + 2 * _nbytes((bm, bn), F32)
    return pl.pallas_call(
        _glu_kernel,
        grid=(m // bm, nj),
        in_specs=[_resident((bm, k), lambda i, j: (i, 0)),
                  _weight(k, bn, layer, lambda j: j),
                  _weight(k, bn, layer, lambda j: j + nj)],
        out_specs=pl.BlockSpec((bm, bn), lambda i, j: (i, j)),
        out_shape=jax.ShapeDtypeStruct((m, n), BF16),
        compiler_params=_params(blk, tmp, ("arbitrary", "arbitrary"), _nbytes((bm, k), a.dtype)),
        name="glu_matmul",
    )(a, w, w)


LOG2E = math.log2(math.e)


def _segment_mean_sq(x, seg_ref):
    sq = x * x
    hi = sq.astype(BF16)
    lo = (sq - hi.astype(F32)).astype(BF16)
    s = (jnp.dot(hi, seg_ref[...], preferred_element_type=F32)
         + jnp.dot(lo, seg_ref[...], preferred_element_type=F32))
    return s * (1.0 / HEAD_DIM)


def _attn_kernel(sink_ref, q_ref, kp_ref, kc_ref, vp_ref, vc_ref, bias_ref, qg_ref, kg_ref,
                 seg_ref, o_ref, s_scr, p_scr, sink_scr):
    lanes = V7X_LANES
    low = lax.broadcasted_iota(jnp.int32, (1, lanes), 1) < HEAD_DIM
    high = jnp.logical_not(low)

    kk = jnp.concatenate([kp_ref[...], kc_ref[...]], axis=0)
    vv = jnp.concatenate([vp_ref[...], vc_ref[...]], axis=0)
    ones = jnp.ones((HEAD_DIM, 2 * BLOCK), F32)

    k_low, k_high, v_low, v_high = [], [], [], []
    for c in range(KV_WIDTH // lanes):
        kc = kk[:, c * lanes:(c + 1) * lanes]
        kc = kc * lax.rsqrt(_segment_mean_sq(kc, seg_ref) + EPS) * kg_ref[:, c * lanes:(c + 1) * lanes]
        ks = pltpu.roll(kc, HEAD_DIM, axis=1)
        vt = vv[:, c * lanes:(c + 1) * lanes].T
        for odd in range(2):
            k_low.append(jnp.where(low, ks if odd else kc, 0.0).astype(BF16))
            k_high.append(jnp.where(high, kc if odd else ks, 0.0).astype(BF16))
            vth = vt[odd * HEAD_DIM:(odd + 1) * HEAD_DIM, :]
            v_low.append(jnp.concatenate([vth, ones], axis=0).astype(BF16))
            v_high.append(jnp.concatenate([ones, vth], axis=0).astype(BF16))

    n_pairs = ATTN_WIDTH // lanes
    for pair in range(n_pairs):
        kv = pair // (Q_GROUP // 2)
        qt = q_ref[:, pair * lanes:(pair + 1) * lanes].T
        ms = [jnp.mean(jnp.square(qt[h * HEAD_DIM:(h + 1) * HEAD_DIM]), axis=0, keepdims=True)
              for h in range(2)]
        rs = jnp.concatenate([jnp.broadcast_to(lax.rsqrt(m + EPS), (HEAD_DIM, BLOCK)) for m in ms], axis=0)
        qt = (qt * rs * qg_ref[...]).astype(BF16)
        for half, kh in enumerate((k_low[kv], k_high[kv])):
            head = 2 * pair + half
            s_scr[head] = jnp.dot(kh, qt, preferred_element_type=F32) + bias_ref[head]

    for head in range(N_Q_HEADS):
        s = s_scr[head]
        sink = sink_ref[head]
        mx = jnp.maximum(jnp.max(s, axis=0, keepdims=True), sink)
        p_scr[head] = jnp.exp2(s - mx).astype(BF16)
        sink_scr[head:head + 1, :] = jnp.exp2(sink - mx)

    for pair in range(n_pairs):
        kv = pair // (Q_GROUP // 2)
        num, den = [], []
        for half, vh in enumerate((v_low[kv], v_high[kv])):
            head = 2 * pair + half
            o = jnp.dot(vh, p_scr[head], preferred_element_type=F32)
            lo_rows, hi_rows = o[:HEAD_DIM], o[HEAD_DIM:]
            num.append(lo_rows if half == 0 else hi_rows)
            den.append((hi_rows if half == 0 else lo_rows) + sink_scr[head:head + 1, :])
        ot = jnp.concatenate(num, axis=0) / jnp.concatenate(den, axis=0)
        o_ref[:, pair * lanes:(pair + 1) * lanes] = ot.T.astype(o_ref.dtype)


def attention(proj, bias, q_gain, k_gain, sinks, seq):
    m = proj.shape[0]
    nb = seq // BLOCK
    nblk = m // BLOCK
    lanes = V7X_LANES
    seg_id = jnp.arange(lanes) // HEAD_DIM
    seg = (seg_id[:, None] == seg_id[None, :]).astype(BF16)
    qg = jnp.broadcast_to(jnp.tile(q_gain.astype(F32) * (HEAD_DIM ** -0.5 * LOG2E), 2)[:, None], (lanes, BLOCK))
    kg = jnp.tile(k_gain.astype(F32), N_KV_HEADS).reshape(1, KV_WIDTH)
    kcol, vcol = OFF_K // KV_WIDTH, OFF_V // KV_WIDTH
    table = (None,) + bias.shape[1:]

    def prev(i):
        return jnp.where(i % nb == 0, i, i - 1)

    blk = (_nbytes((BLOCK, ATTN_WIDTH), F32) + 4 * _nbytes((BLOCK, KV_WIDTH), F32)
           + _nbytes(bias.shape[1:], F32) + _nbytes((BLOCK, ATTN_WIDTH), BF16))
    scratch = (_nbytes((N_Q_HEADS, 2 * BLOCK, BLOCK), F32) + _nbytes((N_Q_HEADS, 2 * BLOCK, BLOCK), BF16)
               + _nbytes((N_Q_HEADS, BLOCK), F32))
    return pl.pallas_call(
        _attn_kernel,
        grid=(nblk,),
        in_specs=[pl.BlockSpec(memory_space=pltpu.SMEM),
                  pl.BlockSpec((BLOCK, ATTN_WIDTH), lambda i: (i, 0)),
                  pl.BlockSpec((BLOCK, KV_WIDTH), lambda i: (prev(i), kcol)),
                  pl.BlockSpec((BLOCK, KV_WIDTH), lambda i: (i, kcol)),
                  pl.BlockSpec((BLOCK, KV_WIDTH), lambda i: (prev(i), vcol)),
                  pl.BlockSpec((BLOCK, KV_WIDTH), lambda i: (i, vcol)),
                  pl.BlockSpec(table, lambda i: (jnp.where(i % nb == 0, 0, 1), 0, 0, 0)),
                  pl.BlockSpec((lanes, BLOCK), lambda i: (0, 0)),
                  pl.BlockSpec((1, KV_WIDTH), lambda i: (0, 0)),
                  pl.BlockSpec((lanes, lanes), lambda i: (0, 0))],
        out_specs=pl.BlockSpec((BLOCK, ATTN_WIDTH), lambda i: (i, 0)),
        out_shape=jax.ShapeDtypeStruct((m, ATTN_WIDTH), BF16),
        scratch_shapes=[pltpu.VMEM((N_Q_HEADS, 2 * BLOCK, BLOCK), F32),
                        pltpu.VMEM((N_Q_HEADS, 2 * BLOCK, BLOCK), BF16),
                        pltpu.VMEM((N_Q_HEADS, BLOCK), F32)],
        compiler_params=_params(blk, scratch, semantics=("arbitrary",)),
        name="swa_attention",
    )(sinks.astype(F32) * LOG2E, proj, proj, proj, proj, proj, bias, qg, kg, seg)


def _t5_bucket(dist):
    max_exact = N_BUCKETS // 2
    n = jnp.maximum(dist, 0)
    ratio = jnp.log(jnp.maximum(n, 1).astype(F32) / max_exact) / math.log(MAX_DISTANCE / max_exact)
    large = jnp.minimum(max_exact + (ratio * (N_BUCKETS - max_exact)).astype(jnp.int32), N_BUCKETS - 1)
    return jnp.where(n < max_exact, n, large)


def band_bias(rel_bias):
    keys, period = 2 * BLOCK, 4 * BLOCK
    by_dist = rel_bias.astype(F32)[_t5_bucket(jnp.arange(WINDOW))].T * LOG2E
    v = jnp.concatenate([by_dist, jnp.full((N_Q_HEADS, period - WINDOW), -jnp.inf, F32)], axis=1)
    a = jnp.tile(v, (1, keys))[:, :keys * (period - 1)].reshape(N_Q_HEADS, keys, period - 1)
    rest = a[:, :, BLOCK:2 * BLOCK]
    c = jnp.arange(keys)[None, :, None]
    first = jnp.where(c >= BLOCK, rest, -jnp.inf)
    return jnp.stack([first, rest])


CONV_HALO = 32
CONV_ROWS = 32


def _conv_kernel(a_ref, g_ref, w_ref, b_ref, lg_ref, lb_ref, pw_ref, o_ref, ubuf, shbuf, ybuf, *, nt, tt):
    t = pl.program_id(0) % nt

    @pl.when(t == 0)
    def _():
        ubuf[0:CONV_HALO, :] = jnp.zeros((CONV_HALO, CONV_WIDTH), F32)

    @pl.when(t != 0)
    def _():
        ubuf[0:CONV_HALO, :] = ubuf[tt:tt + CONV_HALO, :]

    ubuf[CONV_HALO:CONV_HALO + tt, :] = a_ref[...] * jax.nn.sigmoid(g_ref[...])

    first = CONV_HALO - (CONV_KERNEL - 1)
    cw = 2 * V7X_LANES

    span = tt + CONV_HALO - V7X_SUBLANES
    for r in range(1, V7X_SUBLANES):
        shbuf[r - 1, 0:span, :] = ubuf[r:r + span, :]

    def rows(rb, carry):
        base = pl.multiple_of(rb * CONV_ROWS, CONV_ROWS)
        for cb in range(CONV_WIDTH // cw):
            cols = slice(cb * cw, (cb + 1) * cw)
            acc = jnp.zeros((CONV_ROWS, cw), F32) + b_ref[:, cols]
            for j in range(CONV_KERNEL):
                r, a = (first + j) % V7X_SUBLANES, (first + j) // V7X_SUBLANES * V7X_SUBLANES
                tap = (ubuf[pl.ds(base + a, CONV_ROWS), cols] if r == 0
                       else shbuf[r - 1, pl.ds(base + a, CONV_ROWS), cols])
                acc = acc + w_ref[j:j + 1, cols] * tap
            ybuf[pl.ds(base, CONV_ROWS), cols] = acc
        return carry

    lax.fori_loop(0, tt // CONV_ROWS, rows, 0)

    y = ybuf[...]
    mu = jnp.mean(y, axis=-1, keepdims=True)
    yc = y - mu
    var = jnp.mean(yc * yc, axis=-1, keepdims=True)
    z = yc * lax.rsqrt(var + EPS) * lg_ref[...] + lb_ref[...]
    z = z * jax.nn.sigmoid(z)
    o_ref[...] = jnp.dot(z.astype(BF16), pw_ref[...], preferred_element_type=F32).astype(o_ref.dtype)


def conv_branch(proj, w_dw, b_dw, ln_g, ln_b, w_pw, seq, tt=256):
    m = proj.shape[0]
    tt = min(tt, seq)
    nt = seq // tt
    acol = OFF_CONV // CONV_WIDTH
    w_pad = jnp.zeros((CONV_HALO, CONV_WIDTH), F32).at[:CONV_KERNEL].set(w_dw.astype(F32))
    row = lambda v: v.astype(F32).reshape(1, CONV_WIDTH)
    blk = (2 * _nbytes((tt, CONV_WIDTH), F32) + _nbytes((CONV_HALO, CONV_WIDTH), F32)
           + 3 * _nbytes((1, CONV_WIDTH), F32) + _nbytes((CONV_WIDTH, CONV_WIDTH), BF16)
           + _nbytes((tt, CONV_WIDTH), BF16))
    scratch = V7X_SUBLANES * _nbytes((tt + CONV_HALO, CONV_WIDTH), F32) + _nbytes((tt, CONV_WIDTH), F32)
    return pl.pallas_call(
        functools.partial(_conv_kernel, nt=nt, tt=tt),
        grid=(m // tt,),
        in_specs=[pl.BlockSpec((tt, CONV_WIDTH), lambda i: (i, acol)),
                  pl.BlockSpec((tt, CONV_WIDTH), lambda i: (i, acol + 1)),
                  pl.BlockSpec((CONV_HALO, CONV_WIDTH), lambda i: (0, 0)),
                  pl.BlockSpec((1, CONV_WIDTH), lambda i: (0, 0)),
                  pl.BlockSpec((1, CONV_WIDTH), lambda i: (0, 0)),
                  pl.BlockSpec((1, CONV_WIDTH), lambda i: (0, 0)),
                  pl.BlockSpec((CONV_WIDTH, CONV_WIDTH), lambda i: (0, 0))],
        out_specs=pl.BlockSpec((tt, CONV_WIDTH), lambda i: (i, 0)),
        out_shape=jax.ShapeDtypeStruct((m, CONV_WIDTH), BF16),
        scratch_shapes=[pltpu.VMEM((tt + CONV_HALO, CONV_WIDTH), F32),
                        pltpu.VMEM((V7X_SUBLANES - 1, tt + CONV_HALO, CONV_WIDTH), F32),
                        pltpu.VMEM((tt, CONV_WIDTH), F32)],
        compiler_params=_params(blk, scratch, semantics=("arbitrary",)),
        name="conv_branch",
    )(proj, proj, w_pad, row(b_dw), row(ln_g), row(ln_b), w_pw)


def _swap_re_im(h):
    lanes = V7X_LANES
    return jnp.concatenate([pltpu.roll(h[:, k * lanes:(k + 1) * lanes], SSM_STATE, axis=1)
                            for k in range(h.shape[1] // lanes)], axis=1)


def _ssm_kernel(u_ref, ws_ref, wi_ref, wxt_ref, a1_ref, a2_ref, d_ref, o_ref, *, nc):
    x = jnp.concatenate([u_ref[pl.ds(t, nc, stride=SSM_CHUNK), :] for t in range(SSM_CHUNK)], axis=1)
    xb = x.astype(BF16)
    h = jnp.dot(xb, ws_ref[0], preferred_element_type=F32)
    row = lax.broadcasted_iota(jnp.int32, (nc, 1), 0)
    level = 0
    while (1 << level) < nc:
        d = 1 << level
        prev = jnp.where(row >= d, pltpu.roll(h, d, axis=0), 0.0)
        h = h + a1_ref[0, level:level + 1, :] * prev + a2_ref[0, level:level + 1, :] * _swap_re_im(prev)
        level += 1
    hp = jnp.where(row >= 1, pltpu.roll(h, 1, axis=0), 0.0).astype(BF16)
    y = (jnp.dot(xb, wi_ref[0], preferred_element_type=F32)
         + lax.dot_general(hp, wxt_ref[0], (((1,), (1,)), ((), ())), preferred_element_type=F32)
         + d_ref[0] * x)
    y = jax.nn.gelu(y)
    for t in range(SSM_CHUNK):
        o_ref[pl.ds(t, nc, stride=SSM_CHUNK), :] = y[:, t * SSM_QCH:(t + 1) * SSM_QCH]


def ssm_branch(proj, w_state, w_intra, w_inter_t, a1, a2, d_skip, seq):
    m = proj.shape[0]
    nseq = m // seq
    nc = seq // SSM_CHUNK
    ucol = OFF_SSM // SSM_QCH
    kw = SSM_CHUNK * SSM_QCH
    sw = 2 * SSM_QSTATE
    nlev = a1.shape[1]
    d_t = jnp.tile(d_skip.astype(F32).reshape(SSM_NQ, 1, SSM_QCH), (1, 1, SSM_CHUNK))
    blk = (2 * _nbytes((seq, SSM_QCH), F32) + 2 * _nbytes((kw, sw), BF16) + _nbytes((kw, kw), BF16)
           + 2 * _nbytes((nlev, sw), F32) + _nbytes((1, kw), F32))
    tmp = 6 * _nbytes((nc, kw), F32)
    per_q = lambda shape: pl.BlockSpec((1,) + shape, lambda q, b: (q, 0, 0))
    return pl.pallas_call(
        functools.partial(_ssm_kernel, nc=nc),
        grid=(SSM_NQ, nseq),
        in_specs=[pl.BlockSpec((seq, SSM_QCH), lambda q, b: (b, ucol + q)),
                  per_q((kw, sw)), per_q((kw, kw)), per_q((kw, sw)),
                  per_q((nlev, sw)), per_q((nlev, sw)), per_q((1, kw))],
        out_specs=pl.BlockSpec((seq, SSM_QCH), lambda q, b: (b, q)),
        out_shape=jax.ShapeDtypeStruct((m, SSM_WIDTH), F32),
        compiler_params=_params(blk, tmp, semantics=("arbitrary", "arbitrary")),
        name="ssm_scan",
    )(proj, w_state, w_intra, w_inter_t, a1, a2, d_t)


def _cmul(ar, ai, br, bi):
    return ar * br - ai * bi, ar * bi + ai * br


def ssm_operands(a_re, a_im, log_dt, b_re, b_im, c_re, c_im, n_chunks):
    f32 = F32
    dt = jnp.exp(log_dt.astype(f32))[:, None]
    lam_re, lam_im = a_re.astype(f32), a_im.astype(f32)
    mag = jnp.exp(dt * lam_re)
    ang = dt * lam_im
    lb_re, lb_im = mag * jnp.cos(ang), mag * jnp.sin(ang)
    nr = lb_re - 1.0
    den = lam_re * lam_re + lam_im * lam_im
    coef_re = (nr * lam_re + lb_im * lam_im) / den
    coef_im = (lb_im * lam_re - nr * lam_im) / den
    br, bi = b_re.astype(f32), b_im.astype(f32)
    bb_re = coef_re[..., None] * br - coef_im[..., None] * bi
    bb_im = coef_re[..., None] * bi + coef_im[..., None] * br
    cr, ci = c_re.astype(f32), c_im.astype(f32)

    pw = [(jnp.ones_like(lb_re), jnp.zeros_like(lb_im))]
    for _ in range(SSM_CHUNK):
        pw.append(_cmul(pw[-1][0], pw[-1][1], lb_re, lb_im))

    T, Q, C, P = SSM_CHUNK, SSM_QGROUPS, SSM_GROUP, SSM_STATE
    rows_q = Q * C
    same_group = (jnp.arange(rows_q)[:, None] // C == jnp.arange(Q)[None, :])

    def group_diag(blocks):
        w = blocks.shape[-1]
        b = blocks.reshape(SSM_NQ, rows_q, 1, w)
        return jnp.where(same_group[None, :, :, None], b, 0.0).reshape(SSM_NQ, rows_q, Q * w)

    ws = []
    for s in range(T):
        e_re, e_im = _cmul(pw[T - 1 - s][0][..., None], pw[T - 1 - s][1][..., None], bb_re, bb_im)
        e = jnp.concatenate([jnp.swapaxes(e_re, 1, 2), jnp.swapaxes(e_im, 1, 2)], axis=-1)
        ws.append(group_diag(e.astype(BF16)))
    w_state = jnp.concatenate(ws, axis=1)

    wx = []
    for t in range(T):
        e_re, e_im = _cmul(cr, ci, pw[t + 1][0][:, None, :], pw[t + 1][1][:, None, :])
        wx.append(group_diag(jnp.concatenate([e_re, -e_im], axis=-1).astype(BF16)))
    w_inter_t = jnp.concatenate(wx, axis=1)

    ker = []
    for l in range(T):
        e_re, e_im = _cmul(pw[l][0][..., None], pw[l][1][..., None], bb_re, bb_im)
        k_l = (jnp.einsum('gdp,gpc->gcd', cr, e_re, precision=lax.Precision.HIGHEST)
               - jnp.einsum('gdp,gpc->gcd', ci, e_im, precision=lax.Precision.HIGHEST))
        ker.append(group_diag(k_l.astype(BF16)))
    zero = jnp.zeros_like(ker[0])
    w_intra = jnp.concatenate(
        [jnp.concatenate([ker[t - s] if t >= s else zero for t in range(T)], axis=2) for s in range(T)],
        axis=1)

    a = pw[T]
    a1, a2 = [], []
    level = 0
    while (1 << level) < n_chunks:
        a1.append(jnp.concatenate([a[0], a[0]], axis=-1).reshape(SSM_NQ, Q * 2 * P))
        a2.append(jnp.concatenate([-a[1], a[1]], axis=-1).reshape(SSM_NQ, Q * 2 * P))
        a = _cmul(a[0], a[1], a[0], a[1])
        level += 1
    return w_state, w_intra, w_inter_t, jnp.stack(a1, axis=1), jnp.stack(a2, axis=1)


def _merge_kernel(oa_ref, oc_ref, os_ref, ga_ref, gc_ref, gs_ref, wa_ref, wc_ref, ws_ref, o_ref):
    ya = _wdot(oa_ref[...], wa_ref)
    yc = _wdot(oc_ref[...], wc_ref)
    ys = _wdot(os_ref[...], ws_ref)
    merged = (jax.nn.sigmoid(ga_ref[...]) * ya + jax.nn.sigmoid(gc_ref[...]) * yc
              + jax.nn.sigmoid(gs_ref[...]) * ys)
    o_ref[...] = merged.astype(o_ref.dtype)


def branch_merge(o_attn, o_conv, o_ssm, proj, w_branch, layer, bm=2048, bn=256):
    m = proj.shape[0]
    bm = min(bm, m)
    gcol = OFF_GATE // bn
    gstep = D_MODEL // bn
    blk = 3 * _nbytes((bm, bn), F32) + _nbytes((MIX_WIDTH, bn), F32) + _nbytes((bm, bn), BF16)
    tmp = _nbytes((MIX_WIDTH, bn), BF16) + 4 * _nbytes((bm, bn), F32)

    def wrows(rows, row_tile):
        return pl.BlockSpec((None, rows, bn), lambda i, j: (layer, row_tile, j))

    return pl.pallas_call(
        _merge_kernel,
        grid=(m // bm, D_MODEL // bn),
        in_specs=[_resident((bm, ATTN_WIDTH), lambda i, j: (i, 0)),
                  _resident((bm, CONV_WIDTH), lambda i, j: (i, 0)),
                  _resident((bm, SSM_WIDTH), lambda i, j: (i, 0)),
                  pl.BlockSpec((bm, bn), lambda i, j: (i, gcol + j)),
                  pl.BlockSpec((bm, bn), lambda i, j: (i, gcol + gstep + j)),
                  pl.BlockSpec((bm, bn), lambda i, j: (i, gcol + 2 * gstep + j)),
                  wrows(ATTN_WIDTH, 0),
                  wrows(CONV_WIDTH, ATTN_WIDTH // CONV_WIDTH),
                  wrows(SSM_WIDTH, (ATTN_WIDTH + CONV_WIDTH) // SSM_WIDTH)],
        out_specs=pl.BlockSpec((bm, bn), lambda i, j: (i, j)),
        out_shape=jax.ShapeDtypeStruct((m, D_MODEL), BF16),
        compiler_params=_params(blk, tmp, ("arbitrary", "arbitrary"), _nbytes((bm, MIX_WIDTH), BF16)),
        name="branch_merge",
    )(o_attn, o_conv, o_ssm, proj, proj, proj, w_branch, w_branch, w_branch)


FFN_OUT_ROWS = 1024


def _ffn(x, norm_g, w_in, w_out, layer):
    h = rmsnorm(x, norm_g[layer])
    act = swiglu_in(h, w_in, layer)
    return matmul_res(act, w_out, layer, x, 0.5, bm=FFN_OUT_ROWS)


def _layer(x, seq, bias, p, layer):
    at = lambda name: p[name][layer]
    x = _ffn(x, p['ffn1_norm'], p['w_ffn1_in'], p['w_ffn1_out'], layer)
    h = rmsnorm(x, at('mix_norm'))
    proj = matmul(h, p['w_in'], layer, F32)
    o_attn = attention(proj, bias, at('q_norm'), at('k_norm'), at('attn_sinks'), seq)
    o_conv = conv_branch(proj, at('conv_dw'), at('conv_dw_bias'), at('conv_ln_g'), at('conv_ln_b'),
                         at('conv_pw').astype(BF16), seq)
    ops = ssm_operands(at('ssm_a_re'), at('ssm_a_im'), at('ssm_log_dt'), at('ssm_b_re'), at('ssm_b_im'),
                       at('ssm_c_re'), at('ssm_c_im'), seq // SSM_CHUNK)
    y_ssm = ssm_branch(proj, *ops, at('ssm_d'), seq)
    o_ssm = glu_matmul(y_ssm, p['ssm_glu'], layer)
    merged = branch_merge(o_attn, o_conv, o_ssm, proj, p['w_branch'], layer)
    x = matmul_res(merged, p['w_out'], layer, x, 1.0, bm=2048)
    return _ffn(x, p['ffn2_norm'], p['w_ffn2_in'], p['w_ffn2_out'], layer)


def kernel(x, rel_bias, ffn1_norm, w_ffn1_in, w_ffn1_out, mix_norm, w_in, q_norm, k_norm, attn_sinks,
           conv_dw, conv_dw_bias, conv_ln_g, conv_ln_b, conv_pw, ssm_a_re, ssm_a_im, ssm_log_dt,
           ssm_b_re, ssm_b_im, ssm_c_re, ssm_c_im, ssm_d, ssm_glu, w_branch, w_out, ffn2_norm,
           w_ffn2_in, w_ffn2_out):
    layer_params = dict(
        ffn1_norm=ffn1_norm, w_ffn1_in=w_ffn1_in, w_ffn1_out=w_ffn1_out, mix_norm=mix_norm, w_in=w_in,
        q_norm=q_norm, k_norm=k_norm, attn_sinks=attn_sinks, conv_dw=conv_dw, conv_dw_bias=conv_dw_bias,
        conv_ln_g=conv_ln_g, conv_ln_b=conv_ln_b, conv_pw=conv_pw, ssm_a_re=ssm_a_re, ssm_a_im=ssm_a_im,
        ssm_log_dt=ssm_log_dt, ssm_b_re=ssm_b_re, ssm_b_im=ssm_b_im, ssm_c_re=ssm_c_re, ssm_c_im=ssm_c_im,
        ssm_d=ssm_d, ssm_glu=ssm_glu, w_branch=w_branch, w_out=w_out, ffn2_norm=ffn2_norm,
        w_ffn2_in=w_ffn2_in, w_ffn2_out=w_ffn2_out)
    b, seq, d = x.shape
    bias = band_bias(rel_bias)
    xf = x.reshape(b * seq, d)
    for layer in range(DEPTH):
        xf = _layer(xf, seq, bias, layer_params, layer)
    return xf.reshape(b, seq, d)
```

```python
import functools
import math

import jax
import jax.numpy as jnp
from jax import lax
from jax.experimental import pallas as pl
from jax.experimental.pallas import tpu as pltpu

F32 = jnp.float32
BF16 = jnp.bfloat16

D_MODEL = 4096
DEPTH = 2
N_Q_HEADS = 32
N_KV_HEADS = 8
HEAD_DIM = 64
Q_GROUP = N_Q_HEADS // N_KV_HEADS
ATTN_WIDTH = N_Q_HEADS * HEAD_DIM
KV_WIDTH = N_KV_HEADS * HEAD_DIM
WINDOW = 128
BLOCK = 128
N_BUCKETS = 32
MAX_DISTANCE = 128
CONV_WIDTH = D_MODEL // 4
CONV_KERNEL = 31
SSM_WIDTH = D_MODEL // 4
SSM_GROUP = 16
SSM_GROUPS = SSM_WIDTH // SSM_GROUP
SSM_STATE = 64
D_FF = 256 * ((8 * D_MODEL // 3 + 255) // 256)
N_BRANCH = 3
MIX_WIDTH = ATTN_WIDTH + CONV_WIDTH + SSM_WIDTH
IN_WIDTH = ATTN_WIDTH + 2 * KV_WIDTH + 2 * CONV_WIDTH + SSM_WIDTH + N_BRANCH * D_MODEL
EPS = 1e-6

OFF_K = ATTN_WIDTH
OFF_V = OFF_K + KV_WIDTH
OFF_CONV = OFF_V + KV_WIDTH
OFF_SSM = OFF_CONV + 2 * CONV_WIDTH
OFF_GATE = OFF_SSM + SSM_WIDTH

V7X_LANES = 128
V7X_SUBLANES = 8
V7X_VMEM_BYTES = 64 * 1024 * 1024
V7X_VMEM_INTERNAL_BYTES = 12 * 1024 * 1024

SSM_CHUNK = 8
SSM_QGROUPS = 8
SSM_QCH = SSM_QGROUPS * SSM_GROUP
SSM_NQ = SSM_GROUPS // SSM_QGROUPS
SSM_QSTATE = SSM_QGROUPS * SSM_STATE


def _nbytes(shape, dtype):
    return math.prod(shape) * jnp.dtype(dtype).itemsize


def _params(block_bytes, scratch_bytes=0, semantics=None, single_bytes=0):
    limit = 2 * block_bytes + single_bytes + scratch_bytes + V7X_VMEM_INTERNAL_BYTES
    limit = min(limit, V7X_VMEM_BYTES - 4 * 1024 * 1024)
    return pltpu.CompilerParams(dimension_semantics=semantics, vmem_limit_bytes=int(limit))


def _resident(shape, index_map):
    return pl.BlockSpec(shape, index_map, pipeline_mode=pl.Buffered(1))


def _weight(k, bn, layer, col):
    return pl.BlockSpec((None, k, bn), lambda i, j: (layer, 0, col(j)))


def _wdot(a, w_ref):
    return jnp.dot(a, w_ref[...].astype(BF16), preferred_element_type=F32)


def _lane_partial_sq(y):
    sq = y * y
    part = sq[:, :V7X_LANES]
    for c in range(1, y.shape[1] // V7X_LANES):
        part = part + sq[:, c * V7X_LANES:(c + 1) * V7X_LANES]
    return part


def _row_scale(ssq_ref, d):
    return lax.rsqrt(jnp.sum(ssq_ref[...], axis=-1, keepdims=True) * (1.0 / d) + EPS)


def _norm_prep_kernel(x_ref, g_ref, xg_ref, ssq_ref):
    x = x_ref[...]
    xg_ref[...] = (x * g_ref[...]).astype(xg_ref.dtype)
    ssq_ref[...] = _lane_partial_sq(x)


def norm_prep(x, g, bm=256):
    m, d = x.shape
    blk = (_nbytes((bm, d), F32) + _nbytes((bm, d), BF16) + _nbytes((1, d), F32)
           + _nbytes((bm, V7X_LANES), F32))
    return pl.pallas_call(
        _norm_prep_kernel,
        grid=(m // bm,),
        in_specs=[pl.BlockSpec((bm, d), lambda i: (i, 0)),
                  pl.BlockSpec((1, d), lambda i: (0, 0))],
        out_specs=[pl.BlockSpec((bm, d), lambda i: (i, 0)),
                   pl.BlockSpec((bm, V7X_LANES), lambda i: (i, 0))],
        out_shape=[jax.ShapeDtypeStruct((m, d), BF16),
                   jax.ShapeDtypeStruct((m, V7X_LANES), F32)],
        compiler_params=_params(blk, semantics=("arbitrary",)),
        name="norm_prep",
    )(x, g.reshape(1, d))


def _swiglu_kernel(xg_ref, ssq_ref, wg_ref, wu_ref, o_ref, r_scr):
    @pl.when(pl.program_id(1) == 0)
    def _():
        r_scr[...] = _row_scale(ssq_ref, xg_ref.shape[1])

    h = xg_ref[...]
    r = r_scr[...]
    g = r * _wdot(h, wg_ref)
    u = r * _wdot(h, wu_ref)
    o_ref[...] = (g * jax.nn.sigmoid(g) * u).astype(o_ref.dtype)


def swiglu_in(xg, ssq, w, layer, bm=2048, bn=256):
    m, k = xg.shape
    f = w.shape[2] // 2
    bm = min(bm, m)
    nj = f // bn
    blk = 2 * _nbytes((k, bn), F32) + _nbytes((bm, bn), BF16)
    tmp = 2 * _nbytes((k, bn), BF16) + 2 * _nbytes((bm, bn), F32) + _nbytes((bm, V7X_LANES), F32)
    single = _nbytes((bm, k), BF16) + _nbytes((bm, V7X_LANES), F32)
    return pl.pallas_call(
        _swiglu_kernel,
        grid=(m // bm, nj),
        in_specs=[_resident((bm, k), lambda i, j: (i, 0)),
                  _resident((bm, V7X_LANES), lambda i, j: (i, 0)),
                  _weight(k, bn, layer, lambda j: j),
                  _weight(k, bn, layer, lambda j: j + nj)],
        out_specs=pl.BlockSpec((bm, bn), lambda i, j: (i, j)),
        out_shape=jax.ShapeDtypeStruct((m, f), BF16),
        scratch_shapes=[pltpu.VMEM((bm, 1), F32)],
        compiler_params=_params(blk, tmp, ("arbitrary", "arbitrary"), single),
        name="swiglu_in",
    )(xg, ssq, w, w)


def _matmul_kernel(xg_ref, ssq_ref, w_ref, o_ref, r_scr):
    @pl.when(pl.program_id(1) == 0)
    def _():
        r_scr[...] = _row_scale(ssq_ref, xg_ref.shape[1])

    o_ref[...] = (r_scr[...] * _wdot(xg_ref[...], w_ref)).astype(o_ref.dtype)


def matmul(xg, ssq, w, layer, out_dtype, bm=2048, bn=512):
    m, k = xg.shape
    n = w.shape[2]
    bm, bn = min(bm, m), min(bn, n)
    blk = _nbytes((k, bn), F32) + _nbytes((bm, bn), out_dtype)
    tmp = _nbytes((k, bn), BF16) + _nbytes((bm, bn), F32) + _nbytes((bm, V7X_LANES), F32)
    single = _nbytes((bm, k), BF16) + _nbytes((bm, V7X_LANES), F32)
    return pl.pallas_call(
        _matmul_kernel,
        grid=(m // bm, n // bn),
        in_specs=[_resident((bm, k), lambda i, j: (i, 0)),
                  _resident((bm, V7X_LANES), lambda i, j: (i, 0)),
                  _weight(k, bn, layer, lambda j: j)],
        out_specs=pl.BlockSpec((bm, bn), lambda i, j: (i, j)),
        out_shape=jax.ShapeDtypeStruct((m, n), out_dtype),
        scratch_shapes=[pltpu.VMEM((bm, 1), F32)],
        compiler_params=_params(blk, tmp, ("arbitrary", "arbitrary"), single),
        name="matmul",
    )(xg, ssq, w)


def _matmul_res_kernel(a_ref, w_ref, x_ref, o_ref, *, scale):
    o_ref[...] = x_ref[...] + scale * _wdot(a_ref[...], w_ref)


def _matmul_res_norm_kernel(a_ref, w_ref, x_ref, g_ref, o_ref, xg_ref, ssq_ref, *, scale):
    y = x_ref[...] + scale * _wdot(a_ref[...], w_ref)
    o_ref[...] = y
    xg_ref[...] = (y * g_ref[...]).astype(xg_ref.dtype)
    part = _lane_partial_sq(y)

    @pl.when(pl.program_id(1) == 0)
    def _():
        ssq_ref[...] = part

    @pl.when(pl.program_id(1) != 0)
    def _():
        ssq_ref[...] += part


def matmul_res(a, w, layer, x, scale, bm, next_gain=None, bn=256):
    m, k = a.shape
    n = w.shape[2]
    bm, bn = min(bm, m), min(bn, n)
    tile = pl.BlockSpec((bm, bn), lambda i, j: (i, j))
    in_specs = [_resident((bm, k), lambda i, j: (i, 0)), _weight(k, bn, layer, lambda j: j), tile]
    blk = _nbytes((k, bn), F32) + 2 * _nbytes((bm, bn), F32)
    tmp = _nbytes((k, bn), BF16) + _nbytes((bm, bn), F32)
    single = _nbytes((bm, k), a.dtype)
    sem = ("arbitrary", "arbitrary")
    if next_gain is None:
        return pl.pallas_call(
            functools.partial(_matmul_res_kernel, scale=scale),
            grid=(m // bm, n // bn),
            in_specs=in_specs,
            out_specs=tile,
            out_shape=jax.ShapeDtypeStruct((m, n), F32),
            compiler_params=_params(blk, tmp, sem, single),
            name="matmul_res",
        )(a, w, x)
    blk += _nbytes((1, bn), F32) + _nbytes((bm, bn), BF16) + _nbytes((bm, V7X_LANES), F32)
    return pl.pallas_call(
        functools.partial(_matmul_res_norm_kernel, scale=scale),
        grid=(m // bm, n // bn),
        in_specs=in_specs + [pl.BlockSpec((1, bn), lambda i, j: (0, j))],
        out_specs=[tile, tile, pl.BlockSpec((bm, V7X_LANES), lambda i, j: (i, 0))],
        out_shape=[jax.ShapeDtypeStruct((m, n), F32), jax.ShapeDtypeStruct((m, n), BF16),
                   jax.ShapeDtypeStruct((m, V7X_LANES), F32)],
        compiler_params=_params(blk, tmp, sem, single),
        name="matmul_res_norm",
    )(a, w, x, next_gain.astype(F32).reshape(1, n))


def _glu_kernel(a_ref, wa_ref, wb_ref, o_ref):
    a = a_ref[...].astype(BF16)
    o_ref[...] = (_wdot(a, wa_ref) * jax.nn.sigmoid(_wdot(a, wb_ref))).astype(o_ref.dtype)


def glu_matmul(a, w, layer, bm=2048, bn=256):
    m, k = a.shape
    n = w.shape[2] // 2
    bm = min(bm, m)
    nj = n // bn
    blk = 2 * _nbytes((k, bn), F32) + _nbytes((bm, bn), BF16)
    tmp = _nbytes((bm, k), BF16) + 2 * _nbytes((k, bn), BF16) + 2 * _nbytes((bm, bn), F32)
    return pl.pallas_call(
        _glu_kernel,
        grid=(m // bm, nj),
        in_specs=[_resident((bm, k), lambda i, j: (i, 0)),
                  _weight(k, bn, layer, lambda j: j),
                  _weight(k, bn, layer, lambda j: j + nj)],
        out_specs=pl.BlockSpec((bm, bn), lambda i, j: (i, j)),
        out_shape=jax.ShapeDtypeStruct((m, n), BF16),
        compiler_params=_params(blk, tmp, ("arbitrary", "arbitrary"), _nbytes((bm, k), a.dtype)),
        name="glu_matmul",
    )(a, w, w)


LOG2E = math.log2(math.e)


def _segment_mean_sq(x, seg_ref):
    sq = x * x
    hi = sq.astype(BF16)
    lo = (sq - hi.astype(F32)).astype(BF16)
    s = (jnp.dot(hi, seg_ref[...], preferred_element_type=F32)
         + jnp.dot(lo, seg_ref[...], preferred_element_type=F32))
    return s * (1.0 / HEAD_DIM)


def _attn_kernel(sink_ref, q_ref, kp_ref, kc_ref, vp_ref, vc_ref, bias_ref, qg_ref, kg_ref,
                 seg_ref, o_ref, s_scr, p_scr, sink_scr):
    lanes = V7X_LANES
    low = lax.broadcasted_iota(jnp.int32, (1, lanes), 1) < HEAD_DIM
    high = jnp.logical_not(low)

    kk = jnp.concatenate([kp_ref[...], kc_ref[...]], axis=0)
    vv = jnp.concatenate([vp_ref[...], vc_ref[...]], axis=0)
    ones = jnp.ones((HEAD_DIM, 2 * BLOCK), F32)

    k_low, k_high, v_low, v_high = [], [], [], []
    for c in range(KV_WIDTH // lanes):
        kc = kk[:, c * lanes:(c + 1) * lanes]
        kc = kc * lax.rsqrt(_segment_mean_sq(kc, seg_ref) + EPS) * kg_ref[:, c * lanes:(c + 1) * lanes]
        ks = pltpu.roll(kc, HEAD_DIM, axis=1)
        vt = vv[:, c * lanes:(c + 1) * lanes].T
        for odd in range(2):
            k_low.append(jnp.where(low, ks if odd else kc, 0.0).astype(BF16))
            k_high.append(jnp.where(high, kc if odd else ks, 0.0).astype(BF16))
            vth = vt[odd * HEAD_DIM:(odd + 1) * HEAD_DIM, :]
            v_low.append(jnp.concatenate([vth, ones], axis=0).astype(BF16))
            v_high.append(jnp.concatenate([ones, vth], axis=0).astype(BF16))

    n_pairs = ATTN_WIDTH // lanes
    for pair in range(n_pairs):
        kv = pair // (Q_GROUP // 2)
        qt = q_ref[:, pair * lanes:(pair + 1) * lanes].T
        ms = [jnp.mean(jnp.square(qt[h * HEAD_DIM:(h + 1) * HEAD_DIM]), axis=0, keepdims=True)
              for h in range(2)]
        rs = jnp.concatenate([jnp.broadcast_to(lax.rsqrt(m + EPS), (HEAD_DIM, BLOCK)) for m in ms], axis=0)
        qt = (qt * rs * qg_ref[...]).astype(BF16)
        for half, kh in enumerate((k_low[kv], k_high[kv])):
            head = 2 * pair + half
            s_scr[head] = jnp.dot(kh, qt, preferred_element_type=F32) + bias_ref[head]

    for head in range(N_Q_HEADS):
        s = s_scr[head]
        sink = sink_ref[head]
        mx = jnp.maximum(jnp.max(s, axis=0, keepdims=True), sink)
        p_scr[head] = jnp.exp2(s - mx).astype(BF16)
        sink_scr[head:head + 1, :] = jnp.exp2(sink - mx)

    for pair in range(n_pairs):
        kv = pair // (Q_GROUP // 2)
        num, den = [], []
        for half, vh in enumerate((v_low[kv], v_high[kv])):
            head = 2 * pair + half
            o = jnp.dot(vh, p_scr[head], preferred_element_type=F32)
            lo_rows, hi_rows = o[:HEAD_DIM], o[HEAD_DIM:]
            num.append(lo_rows if half == 0 else hi_rows)
            den.append((hi_rows if half == 0 else lo_rows) + sink_scr[head:head + 1, :])
        ot = jnp.concatenate(num, axis=0) / jnp.concatenate(den, axis=0)
        o_ref[:, pair * lanes:(pair + 1) * lanes] = ot.T.astype(o_ref.dtype)


def attention(proj, bias, q_gain, k_gain, sinks, seq):
    m = proj.shape[0]
    nb = seq // BLOCK
    nblk = m // BLOCK
    lanes = V7X_LANES
    seg_id = jnp.arange(lanes) // HEAD_DIM
    seg = (seg_id[:, None] == seg_id[None, :]).astype(BF16)
    qg = jnp.broadcast_to(jnp.tile(q_gain.astype(F32) * (HEAD_DIM ** -0.5 * LOG2E), 2)[:, None], (lanes, BLOCK))
    kg = jnp.tile(k_gain.astype(F32), N_KV_HEADS).reshape(1, KV_WIDTH)
    kcol, vcol = OFF_K // KV_WIDTH, OFF_V // KV_WIDTH
    table = (None,) + bias.shape[1:]

    def prev(i):
        return jnp.where(i % nb == 0, i, i - 1)

    blk = (_nbytes((BLOCK, ATTN_WIDTH), F32) + 4 * _nbytes((BLOCK, KV_WIDTH), F32)
           + _nbytes(bias.shape[1:], F32) + _nbytes((BLOCK, ATTN_WIDTH), BF16))
    scratch = (_nbytes((N_Q_HEADS, 2 * BLOCK, BLOCK), F32) + _nbytes((N_Q_HEADS, 2 * BLOCK, BLOCK), BF16)
               + _nbytes((N_Q_HEADS, BLOCK), F32))
    return pl.pallas_call(
        _attn_kernel,
        grid=(nblk,),
        in_specs=[pl.BlockSpec(memory_space=pltpu.SMEM),
                  pl.BlockSpec((BLOCK, ATTN_WIDTH), lambda i: (i, 0)),
                  pl.BlockSpec((BLOCK, KV_WIDTH), lambda i: (prev(i), kcol)),
                  pl.BlockSpec((BLOCK, KV_WIDTH), lambda i: (i, kcol)),
                  pl.BlockSpec((BLOCK, KV_WIDTH), lambda i: (prev(i), vcol)),
                  pl.BlockSpec((BLOCK, KV_WIDTH), lambda i: (i, vcol)),
                  pl.BlockSpec(table, lambda i: (jnp.where(i % nb == 0, 0, 1), 0, 0, 0)),
                  pl.BlockSpec((lanes, BLOCK), lambda i: (0, 0)),
                  pl.BlockSpec((1, KV_WIDTH), lambda i: (0, 0)),
                  pl.BlockSpec((lanes, lanes), lambda i: (0, 0))],
        out_specs=pl.BlockSpec((BLOCK, ATTN_WIDTH), lambda i: (i, 0)),
        out_shape=jax.ShapeDtypeStruct((m, ATTN_WIDTH), BF16),
        scratch_shapes=[pltpu.VMEM((N_Q_HEADS, 2 * BLOCK, BLOCK), F32),
                        pltpu.VMEM((N_Q_HEADS, 2 * BLOCK, BLOCK), BF16),
                        pltpu.VMEM((N_Q_HEADS, BLOCK), F32)],
        compiler_params=_params(blk, scratch, semantics=("arbitrary",)),
        name="swa_attention",
    )(sinks.astype(F32) * LOG2E, proj, proj, proj, proj, proj, bias, qg, kg, seg)


def _t5_bucket(dist):
    max_exact = N_BUCKETS // 2
    n = jnp.maximum(dist, 0)
    ratio = jnp.log(jnp.maximum(n, 1).astype(F32) / max_exact) / math.log(MAX_DISTANCE / max_exact)
    large = jnp.minimum(max_exact + (ratio * (N_BUCKETS - max_exact)).astype(jnp.int32), N_BUCKETS - 1)
    return jnp.where(n < max_exact, n, large)


def band_bias(rel_bias):
    keys, period = 2 * BLOCK, 4 * BLOCK
    by_dist = rel_bias.astype(F32)[_t5_bucket(jnp.arange(WINDOW))].T * LOG2E
    v = jnp.concatenate([by_dist, jnp.full((N_Q_HEADS, period - WINDOW), -jnp.inf, F32)], axis=1)
    a = jnp.tile(v, (1, keys))[:, :keys * (period - 1)].reshape(N_Q_HEADS, keys, period - 1)
    rest = a[:, :, BLOCK:2 * BLOCK]
    c = jnp.arange(keys)[None, :, None]
    first = jnp.where(c >= BLOCK, rest, -jnp.inf)
    return jnp.stack([first, rest])


CONV_HALO = 32
CONV_ROWS = 32


def _conv_kernel(a_ref, g_ref, w_ref, b_ref, lg_ref, lb_ref, pw_ref, o_ref, ubuf, shbuf, ybuf, *, nt, tt):
    t = pl.program_id(0) % nt

    @pl.when(t == 0)
    def _():
        ubuf[0:CONV_HALO, :] = jnp.zeros((CONV_HALO, CONV_WIDTH), F32)

    @pl.when(t != 0)
    def _():
        ubuf[0:CONV_HALO, :] = ubuf[tt:tt + CONV_HALO, :]

    ubuf[CONV_HALO:CONV_HALO + tt, :] = a_ref[...] * jax.nn.sigmoid(g_ref[...])

    first = CONV_HALO - (CONV_KERNEL - 1)
    cw = 2 * V7X_LANES

    span = tt + CONV_HALO - V7X_SUBLANES
    for r in range(1, V7X_SUBLANES):
        shbuf[r - 1, 0:span, :] = ubuf[r:r + span, :]

    def rows(rb, carry):
        base = pl.multiple_of(rb * CONV_ROWS, CONV_ROWS)
        for cb in range(CONV_WIDTH // cw):
            cols = slice(cb * cw, (cb + 1) * cw)
            acc = jnp.zeros((CONV_ROWS, cw), F32) + b_ref[:, cols]
            for j in range(CONV_KERNEL):
                r, a = (first + j) % V7X_SUBLANES, (first + j) // V7X_SUBLANES * V7X_SUBLANES
                tap = (ubuf[pl.ds(base + a, CONV_ROWS), cols] if r == 0
                       else shbuf[r - 1, pl.ds(base + a, CONV_ROWS), cols])
                acc = acc + w_ref[j:j + 1, cols] * tap
            ybuf[pl.ds(base, CONV_ROWS), cols] = acc
        return carry

    lax.fori_loop(0, tt // CONV_ROWS, rows, 0)

    y = ybuf[...]
    mu = jnp.mean(y, axis=-1, keepdims=True)
    yc = y - mu
    var = jnp.mean(yc * yc, axis=-1, keepdims=True)
    z = yc * lax.rsqrt(var + EPS) * lg_ref[...] + lb_ref[...]
    z = z * jax.nn.sigmoid(z)
    o_ref[...] = jnp.dot(z.astype(BF16), pw_ref[...], preferred_element_type=F32).astype(o_ref.dtype)


def conv_branch(proj, w_dw, b_dw, ln_g, ln_b, w_pw, seq, tt=256):
    m = proj.shape[0]
    tt = min(tt, seq)
    nt = seq // tt
    acol = OFF_CONV // CONV_WIDTH
    w_pad = jnp.zeros((CONV_HALO, CONV_WIDTH), F32).at[:CONV_KERNEL].set(w_dw.astype(F32))
    row = lambda v: v.astype(F32).reshape(1, CONV_WIDTH)
    blk = (2 * _nbytes((tt, CONV_WIDTH), F32) + _nbytes((CONV_HALO, CONV_WIDTH), F32)
           + 3 * _nbytes((1, CONV_WIDTH), F32) + _nbytes((CONV_WIDTH, CONV_WIDTH), BF16)
           + _nbytes((tt, CONV_WIDTH), BF16))
    scratch = V7X_SUBLANES * _nbytes((tt + CONV_HALO, CONV_WIDTH), F32) + _nbytes((tt, CONV_WIDTH), F32)
    return pl.pallas_call(
        functools.partial(_conv_kernel, nt=nt, tt=tt),
        grid=(m // tt,),
        in_specs=[pl.BlockSpec((tt, CONV_WIDTH), lambda i: (i, acol)),
                  pl.BlockSpec((tt, CONV_WIDTH), lambda i: (i, acol + 1)),
                  pl.BlockSpec((CONV_HALO, CONV_WIDTH), lambda i: (0, 0)),
                  pl.BlockSpec((1, CONV_WIDTH), lambda i: (0, 0)),
                  pl.BlockSpec((1, CONV_WIDTH), lambda i: (0, 0)),
                  pl.BlockSpec((1, CONV_WIDTH), lambda i: (0, 0)),
                  pl.BlockSpec((CONV_WIDTH, CONV_WIDTH), lambda i: (0, 0))],
        out_specs=pl.BlockSpec((tt, CONV_WIDTH), lambda i: (i, 0)),
        out_shape=jax.ShapeDtypeStruct((m, CONV_WIDTH), BF16),
        scratch_shapes=[pltpu.VMEM((tt + CONV_HALO, CONV_WIDTH), F32),
                        pltpu.VMEM((V7X_SUBLANES - 1, tt + CONV_HALO, CONV_WIDTH), F32),
                        pltpu.VMEM((tt, CONV_WIDTH), F32)],
        compiler_params=_params(blk, scratch, semantics=("arbitrary",)),
        name="conv_branch",
    )(proj, proj, w_pad, row(b_dw), row(ln_g), row(ln_b), w_pw)


def _swap_re_im(h):
    lanes = V7X_LANES
    return jnp.concatenate([pltpu.roll(h[:, k * lanes:(k + 1) * lanes], SSM_STATE, axis=1)
                            for k in range(h.shape[1] // lanes)], axis=1)


def _ssm_kernel(u_ref, ws_ref, wi_ref, wxt_ref, a1_ref, a2_ref, d_ref, o_ref, *, nc):
    x = jnp.concatenate([u_ref[pl.ds(t, nc, stride=SSM_CHUNK), :] for t in range(SSM_CHUNK)], axis=1)
    xb = x.astype(BF16)
    h = jnp.dot(xb, ws_ref[0], preferred_element_type=F32)
    row = lax.broadcasted_iota(jnp.int32, (nc, 1), 0)
    level = 0
    while (1 << level) < nc:
        d = 1 << level
        prev = jnp.where(row >= d, pltpu.roll(h, d, axis=0), 0.0)
        h = h + a1_ref[0, level:level + 1, :] * prev + a2_ref[0, level:level + 1, :] * _swap_re_im(prev)
        level += 1
    hp = jnp.where(row >= 1, pltpu.roll(h, 1, axis=0), 0.0).astype(BF16)
    y = (jnp.dot(xb, wi_ref[0], preferred_element_type=F32)
         + lax.dot_general(hp, wxt_ref[0], (((1,), (1,)), ((), ())), preferred_element_type=F32)
         + d_ref[0] * x)
    y = jax.nn.gelu(y)
    for t in range(SSM_CHUNK):
        o_ref[pl.ds(t, nc, stride=SSM_CHUNK), :] = y[:, t * SSM_QCH:(t + 1) * SSM_QCH]


def ssm_branch(proj, operands, d_skip, layer, seq):
    w_state, w_intra, w_inter_t, a1, a2 = operands
    m = proj.shape[0]
    nseq = m // seq
    nc = seq // SSM_CHUNK
    ucol = OFF_SSM // SSM_QCH
    kw = SSM_CHUNK * SSM_QCH
    sw = 2 * SSM_QSTATE
    nlev = a1.shape[2]
    d_t = jnp.tile(d_skip.astype(F32).reshape(-1, SSM_NQ, 1, SSM_QCH), (1, 1, 1, SSM_CHUNK))
    blk = (2 * _nbytes((seq, SSM_QCH), F32) + 2 * _nbytes((kw, sw), BF16) + _nbytes((kw, kw), BF16)
           + 2 * _nbytes((nlev, sw), F32) + _nbytes((1, kw), F32))
    tmp = 6 * _nbytes((nc, kw), F32)
    per_q = lambda shape: pl.BlockSpec((None, 1) + shape, lambda q, b: (layer, q, 0, 0))
    return pl.pallas_call(
        functools.partial(_ssm_kernel, nc=nc),
        grid=(SSM_NQ, nseq),
        in_specs=[pl.BlockSpec((seq, SSM_QCH), lambda q, b: (b, ucol + q)),
                  per_q((kw, sw)), per_q((kw, kw)), per_q((kw, sw)),
                  per_q((nlev, sw)), per_q((nlev, sw)), per_q((1, kw))],
        out_specs=pl.BlockSpec((seq, SSM_QCH), lambda q, b: (b, q)),
        out_shape=jax.ShapeDtypeStruct((m, SSM_WIDTH), F32),
        compiler_params=_params(blk, tmp, semantics=("arbitrary", "arbitrary")),
        name="ssm_scan",
    )(proj, w_state, w_intra, w_inter_t, a1, a2, d_t)


def _cmul(ar, ai, br, bi):
    return ar * br - ai * bi, ar * bi + ai * br


def ssm_operands(a_re, a_im, log_dt, b_re, b_im, c_re, c_im, n_chunks):
    f32 = F32
    dt = jnp.exp(log_dt.astype(f32))[:, None]
    lam_re, lam_im = a_re.astype(f32), a_im.astype(f32)
    mag = jnp.exp(dt * lam_re)
    ang = dt * lam_im
    lb_re, lb_im = mag * jnp.cos(ang), mag * jnp.sin(ang)
    nr = lb_re - 1.0
    den = lam_re * lam_re + lam_im * lam_im
    coef_re = (nr * lam_re + lb_im * lam_im) / den
    coef_im = (lb_im * lam_re - nr * lam_im) / den
    br, bi = b_re.astype(f32), b_im.astype(f32)
    bb_re = coef_re[..., None] * br - coef_im[..., None] * bi
    bb_im = coef_re[..., None] * bi + coef_im[..., None] * br
    cr, ci = c_re.astype(f32), c_im.astype(f32)

    pw = [(jnp.ones_like(lb_re), jnp.zeros_like(lb_im))]
    for _ in range(SSM_CHUNK):
        pw.append(_cmul(pw[-1][0], pw[-1][1], lb_re, lb_im))

    T, Q, C, P = SSM_CHUNK, SSM_QGROUPS, SSM_GROUP, SSM_STATE
    rows_q = Q * C
    same_group = (jnp.arange(rows_q)[:, None] // C == jnp.arange(Q)[None, :])

    def group_diag(blocks):
        w = blocks.shape[-1]
        b = blocks.reshape(SSM_NQ, rows_q, 1, w)
        return jnp.where(same_group[None, :, :, None], b, 0.0).reshape(SSM_NQ, rows_q, Q * w)

    ws = []
    for s in range(T):
        e_re, e_im = _cmul(pw[T - 1 - s][0][..., None], pw[T - 1 - s][1][..., None], bb_re, bb_im)
        e = jnp.concatenate([jnp.swapaxes(e_re, 1, 2), jnp.swapaxes(e_im, 1, 2)], axis=-1)
        ws.append(group_diag(e.astype(BF16)))
    w_state = jnp.concatenate(ws, axis=1)

    wx = []
    for t in range(T):
        e_re, e_im = _cmul(cr, ci, pw[t + 1][0][:, None, :], pw[t + 1][1][:, None, :])
        wx.append(group_diag(jnp.concatenate([e_re, -e_im], axis=-1).astype(BF16)))
    w_inter_t = jnp.concatenate(wx, axis=1)

    ker = []
    for l in range(T):
        e_re, e_im = _cmul(pw[l][0][..., None], pw[l][1][..., None], bb_re, bb_im)
        k_l = (jnp.einsum('gdp,gpc->gcd', cr, e_re, precision=lax.Precision.HIGHEST)
               - jnp.einsum('gdp,gpc->gcd', ci, e_im, precision=lax.Precision.HIGHEST))
        ker.append(group_diag(k_l.astype(BF16)))
    zero = jnp.zeros_like(ker[0])
    w_intra = jnp.concatenate(
        [jnp.concatenate([ker[t - s] if t >= s else zero for t in range(T)], axis=2) for s in range(T)],
        axis=1)

    a = pw[T]
    a1, a2 = [], []
    level = 0
    while (1 << level) < n_chunks:
        a1.append(jnp.concatenate([a[0], a[0]], axis=-1).reshape(SSM_NQ, Q * 2 * P))
        a2.append(jnp.concatenate([-a[1], a[1]], axis=-1).reshape(SSM_NQ, Q * 2 * P))
        a = _cmul(a[0], a[1], a[0], a[1])
        level += 1
    return w_state, w_intra, w_inter_t, jnp.stack(a1, axis=1), jnp.stack(a2, axis=1)


def _merge_kernel(oa_ref, oc_ref, os_ref, ga_ref, gc_ref, gs_ref, wa_ref, wc_ref, ws_ref, o_ref):
    ya = _wdot(oa_ref[...], wa_ref)
    yc = _wdot(oc_ref[...], wc_ref)
    ys = _wdot(os_ref[...], ws_ref)
    merged = (jax.nn.sigmoid(ga_ref[...]) * ya + jax.nn.sigmoid(gc_ref[...]) * yc
              + jax.nn.sigmoid(gs_ref[...]) * ys)
    o_ref[...] = merged.astype(o_ref.dtype)


def branch_merge(o_attn, o_conv, o_ssm, proj, w_branch, layer, bm=2048, bn=256):
    m = proj.shape[0]
    bm = min(bm, m)
    gcol = OFF_GATE // bn
    gstep = D_MODEL // bn
    blk = 3 * _nbytes((bm, bn), F32) + _nbytes((MIX_WIDTH, bn), F32) + _nbytes((bm, bn), BF16)
    tmp = _nbytes((MIX_WIDTH, bn), BF16) + 4 * _nbytes((bm, bn), F32)

    def wrows(rows, row_tile):
        return pl.BlockSpec((None, rows, bn), lambda i, j: (layer, row_tile, j))

    return pl.pallas_call(
        _merge_kernel,
        grid=(m // bm, D_MODEL // bn),
        in_specs=[_resident((bm, ATTN_WIDTH), lambda i, j: (i, 0)),
                  _resident((bm, CONV_WIDTH), lambda i, j: (i, 0)),
                  _resident((bm, SSM_WIDTH), lambda i, j: (i, 0)),
                  pl.BlockSpec((bm, bn), lambda i, j: (i, gcol + j)),
                  pl.BlockSpec((bm, bn), lambda i, j: (i, gcol + gstep + j)),
                  pl.BlockSpec((bm, bn), lambda i, j: (i, gcol + 2 * gstep + j)),
                  wrows(ATTN_WIDTH, 0),
                  wrows(CONV_WIDTH, ATTN_WIDTH // CONV_WIDTH),
                  wrows(SSM_WIDTH, (ATTN_WIDTH + CONV_WIDTH) // SSM_WIDTH)],
        out_specs=pl.BlockSpec((bm, bn), lambda i, j: (i, j)),
        out_shape=jax.ShapeDtypeStruct((m, D_MODEL), BF16),
        compiler_params=_params(blk, tmp, ("arbitrary", "arbitrary"), _nbytes((bm, MIX_WIDTH), BF16)),
        name="branch_merge",
    )(o_attn, o_conv, o_ssm, proj, proj, proj, w_branch, w_branch, w_branch)


FFN_OUT_ROWS = 1024


def _ffn(x, normed, w_in, w_out, layer, next_gain):
    act = swiglu_in(*normed, w_in, layer)
    return matmul_res(act, w_out, layer, x, 0.5, bm=FFN_OUT_ROWS, next_gain=next_gain)


def _layer(x, normed, seq, bias, ssm_ops, p, layer):
    at = lambda name: p[name][layer]
    x, *normed = _ffn(x, normed, p['w_ffn1_in'], p['w_ffn1_out'], layer, at('mix_norm'))
    proj = matmul(*normed, p['w_in'], layer, F32)
    o_attn = attention(proj, bias, at('q_norm'), at('k_norm'), at('attn_sinks'), seq)
    o_conv = conv_branch(proj, at('conv_dw'), at('conv_dw_bias'), at('conv_ln_g'), at('conv_ln_b'),
                         at('conv_pw').astype(BF16), seq)
    y_ssm = ssm_branch(proj, ssm_ops, p['ssm_d'], layer, seq)
    o_ssm = glu_matmul(y_ssm, p['ssm_glu'], layer)
    merged = branch_merge(o_attn, o_conv, o_ssm, proj, p['w_branch'], layer)
    x, *normed = matmul_res(merged, p['w_out'], layer, x, 1.0, bm=2048, next_gain=at('ffn2_norm'))
    if layer + 1 == DEPTH:
        return _ffn(x, normed, p['w_ffn2_in'], p['w_ffn2_out'], layer, None), None
    x, *normed = _ffn(x, normed, p['w_ffn2_in'], p['w_ffn2_out'], layer, p['ffn1_norm'][layer + 1])
    return x, normed


def kernel(x, rel_bias, ffn1_norm, w_ffn1_in, w_ffn1_out, mix_norm, w_in, q_norm, k_norm, attn_sinks,
           conv_dw, conv_dw_bias, conv_ln_g, conv_ln_b, conv_pw, ssm_a_re, ssm_a_im, ssm_log_dt,
           ssm_b_re, ssm_b_im, ssm_c_re, ssm_c_im, ssm_d, ssm_glu, w_branch, w_out, ffn2_norm,
           w_ffn2_in, w_ffn2_out):
    layer_params = dict(
        ffn1_norm=ffn1_norm, w_ffn1_in=w_ffn1_in, w_ffn1_out=w_ffn1_out, mix_norm=mix_norm, w_in=w_in,
        q_norm=q_norm, k_norm=k_norm, attn_sinks=attn_sinks, conv_dw=conv_dw, conv_dw_bias=conv_dw_bias,
        conv_ln_g=conv_ln_g, conv_ln_b=conv_ln_b, conv_pw=conv_pw, ssm_a_re=ssm_a_re, ssm_a_im=ssm_a_im,
        ssm_log_dt=ssm_log_dt, ssm_b_re=ssm_b_re, ssm_b_im=ssm_b_im, ssm_c_re=ssm_c_re, ssm_c_im=ssm_c_im,
        ssm_d=ssm_d, ssm_glu=ssm_glu, w_branch=w_branch, w_out=w_out, ffn2_norm=ffn2_norm,
        w_ffn2_in=w_ffn2_in, w_ffn2_out=w_ffn2_out)
    b, seq, d = x.shape
    bias = band_bias(rel_bias)
    ssm_ops = jax.vmap(functools.partial(ssm_operands, n_chunks=seq // SSM_CHUNK))(
        ssm_a_re, ssm_a_im, ssm_log_dt, ssm_b_re, ssm_b_im, ssm_c_re, ssm_c_im)
    xf = x.reshape(b * seq, d)
    normed = norm_prep(xf, ffn1_norm[0])
    for layer in range(DEPTH):
        xf, normed = _layer(xf, normed, seq, bias, ssm_ops, layer_params, layer)
    return xf.reshape(b, seq, d)
```

```python
import functools
import math

import jax
import jax.numpy as jnp
from jax import lax
from jax.experimental import pallas as pl
from jax.experimental.pallas import tpu as pltpu

F32 = jnp.float32
BF16 = jnp.bfloat16

D_MODEL = 4096
DEPTH = 2
N_Q_HEADS = 32
N_KV_HEADS = 8
HEAD_DIM = 64
Q_GROUP = N_Q_HEADS // N_KV_HEADS
ATTN_WIDTH = N_Q_HEADS * HEAD_DIM
KV_WIDTH = N_KV_HEADS * HEAD_DIM
WINDOW = 128
BLOCK = 128
N_BUCKETS = 32
MAX_DISTANCE = 128
CONV_WIDTH = D_MODEL // 4
CONV_KERNEL = 31
SSM_WIDTH = D_MODEL // 4
SSM_GROUP = 16
SSM_GROUPS = SSM_WIDTH // SSM_GROUP
SSM_STATE = 64
D_FF = 256 * ((8 * D_MODEL // 3 + 255) // 256)
N_BRANCH = 3
MIX_WIDTH = ATTN_WIDTH + CONV_WIDTH + SSM_WIDTH
IN_WIDTH = ATTN_WIDTH + 2 * KV_WIDTH + 2 * CONV_WIDTH + SSM_WIDTH + N_BRANCH * D_MODEL
EPS = 1e-6

OFF_K = ATTN_WIDTH
OFF_V = OFF_K + KV_WIDTH
OFF_CONV = OFF_V + KV_WIDTH
OFF_SSM = OFF_CONV + 2 * CONV_WIDTH
OFF_GATE = OFF_SSM + SSM_WIDTH

V7X_LANES = 128
V7X_SUBLANES = 8
V7X_VMEM_BYTES = 64 * 1024 * 1024
V7X_VMEM_INTERNAL_BYTES = 12 * 1024 * 1024

SSM_CHUNK = 8
SSM_QGROUPS = 8
SSM_QCH = SSM_QGROUPS * SSM_GROUP
SSM_NQ = SSM_GROUPS // SSM_QGROUPS
SSM_QSTATE = SSM_QGROUPS * SSM_STATE


def _nbytes(shape, dtype):
    return math.prod(shape) * jnp.dtype(dtype).itemsize


def _params(block_bytes, scratch_bytes=0, semantics=None, single_bytes=0):
    limit = 2 * block_bytes + single_bytes + scratch_bytes + V7X_VMEM_INTERNAL_BYTES
    limit = min(limit, V7X_VMEM_BYTES - 4 * 1024 * 1024)
    return pltpu.CompilerParams(dimension_semantics=semantics, vmem_limit_bytes=int(limit))


def _resident(shape, index_map):
    return pl.BlockSpec(shape, index_map, pipeline_mode=pl.Buffered(1))


def _weight(k, bn, layer, col):
    return pl.BlockSpec((None, k, bn), lambda i, j: (layer, 0, col(j)))


def _wdot(a, w_ref):
    return jnp.dot(a, w_ref[...].astype(BF16), preferred_element_type=F32)


def _lane_partial_sq(y):
    sq = y * y
    part = sq[:, :V7X_LANES]
    for c in range(1, y.shape[1] // V7X_LANES):
        part = part + sq[:, c * V7X_LANES:(c + 1) * V7X_LANES]
    return part


def _row_scale(ssq_ref, d):
    return lax.rsqrt(jnp.sum(ssq_ref[...], axis=-1, keepdims=True) * (1.0 / d) + EPS)


def _norm_prep_kernel(x_ref, g_ref, xg_ref, ssq_ref):
    x = x_ref[...]
    xg_ref[...] = (x * g_ref[...]).astype(xg_ref.dtype)
    ssq_ref[...] = _lane_partial_sq(x)


def norm_prep(x, g, bm=256):
    m, d = x.shape
    blk = (_nbytes((bm, d), F32) + _nbytes((bm, d), BF16) + _nbytes((1, d), F32)
           + _nbytes((bm, V7X_LANES), F32))
    return pl.pallas_call(
        _norm_prep_kernel,
        grid=(m // bm,),
        in_specs=[pl.BlockSpec((bm, d), lambda i: (i, 0)),
                  pl.BlockSpec((1, d), lambda i: (0, 0))],
        out_specs=[pl.BlockSpec((bm, d), lambda i: (i, 0)),
                   pl.BlockSpec((bm, V7X_LANES), lambda i: (i, 0))],
        out_shape=[jax.ShapeDtypeStruct((m, d), BF16),
                   jax.ShapeDtypeStruct((m, V7X_LANES), F32)],
        compiler_params=_params(blk, semantics=("arbitrary",)),
        name="norm_prep",
    )(x, g.reshape(1, d))


def _swiglu_kernel(xg_ref, ssq_ref, wg_ref, wu_ref, o_ref, r_scr):
    @pl.when(pl.program_id(1) == 0)
    def _():
        r_scr[...] = _row_scale(ssq_ref, xg_ref.shape[1])

    h = xg_ref[...]
    r = r_scr[...]
    g = r * _wdot(h, wg_ref)
    u = r * _wdot(h, wu_ref)
    o_ref[...] = (g * jax.nn.sigmoid(g) * u).astype(o_ref.dtype)


def swiglu_in(xg, ssq, w, layer, bm=2048, bn=256):
    m, k = xg.shape
    f = w.shape[2] // 2
    bm = min(bm, m)
    nj = f // bn
    blk = 2 * _nbytes((k, bn), F32) + _nbytes((bm, bn), BF16)
    tmp = 2 * _nbytes((k, bn), BF16) + 2 * _nbytes((bm, bn), F32) + _nbytes((bm, V7X_LANES), F32)
    single = _nbytes((bm, k), BF16) + _nbytes((bm, V7X_LANES), F32)
    return pl.pallas_call(
        _swiglu_kernel,
        grid=(m // bm, nj),
        in_specs=[_resident((bm, k), lambda i, j: (i, 0)),
                  _resident((bm, V7X_LANES), lambda i, j: (i, 0)),
                  _weight(k, bn, layer, lambda j: j),
                  _weight(k, bn, layer, lambda j: j + nj)],
        out_specs=pl.BlockSpec((bm, bn), lambda i, j: (i, j)),
        out_shape=jax.ShapeDtypeStruct((m, f), BF16),
        scratch_shapes=[pltpu.VMEM((bm, 1), F32)],
        compiler_params=_params(blk, tmp, ("arbitrary", "arbitrary"), single),
        name="swiglu_in",
    )(xg, ssq, w, w)


def _matmul_kernel(xg_ref, ssq_ref, w_ref, o_ref, r_scr):
    @pl.when(pl.program_id(1) == 0)
    def _():
        r_scr[...] = _row_scale(ssq_ref, xg_ref.shape[1])

    o_ref[...] = (r_scr[...] * _wdot(xg_ref[...], w_ref)).astype(o_ref.dtype)


def matmul(xg, ssq, w, layer, out_dtype, bm=2048, bn=512):
    m, k = xg.shape
    n = w.shape[2]
    bm, bn = min(bm, m), min(bn, n)
    blk = _nbytes((k, bn), F32) + _nbytes((bm, bn), out_dtype)
    tmp = _nbytes((k, bn), BF16) + _nbytes((bm, bn), F32) + _nbytes((bm, V7X_LANES), F32)
    single = _nbytes((bm, k), BF16) + _nbytes((bm, V7X_LANES), F32)
    return pl.pallas_call(
        _matmul_kernel,
        grid=(m // bm, n // bn),
        in_specs=[_resident((bm, k), lambda i, j: (i, 0)),
                  _resident((bm, V7X_LANES), lambda i, j: (i, 0)),
                  _weight(k, bn, layer, lambda j: j)],
        out_specs=pl.BlockSpec((bm, bn), lambda i, j: (i, j)),
        out_shape=jax.ShapeDtypeStruct((m, n), out_dtype),
        scratch_shapes=[pltpu.VMEM((bm, 1), F32)],
        compiler_params=_params(blk, tmp, ("arbitrary", "arbitrary"), single),
        name="matmul",
    )(xg, ssq, w)


def _matmul_res_kernel(a_ref, w_ref, x_ref, o_ref, *, scale):
    o_ref[...] = x_ref[...] + scale * _wdot(a_ref[...], w_ref)


def _matmul_res_norm_kernel(a_ref, w_ref, x_ref, g_ref, o_ref, xg_ref, ssq_ref, *, scale):
    y = x_ref[...] + scale * _wdot(a_ref[...], w_ref)
    o_ref[...] = y
    xg_ref[...] = (y * g_ref[...]).astype(xg_ref.dtype)
    part = _lane_partial_sq(y)

    @pl.when(pl.program_id(1) == 0)
    def _():
        ssq_ref[...] = part

    @pl.when(pl.program_id(1) != 0)
    def _():
        ssq_ref[...] += part


def matmul_res(a, w, layer, x, scale, bm, next_gain=None, bn=256):
    m, k = a.shape
    n = w.shape[2]
    bm, bn = min(bm, m), min(bn, n)
    tile = pl.BlockSpec((bm, bn), lambda i, j: (i, j))
    in_specs = [_resident((bm, k), lambda i, j: (i, 0)), _weight(k, bn, layer, lambda j: j), tile]
    blk = _nbytes((k, bn), F32) + 2 * _nbytes((bm, bn), F32)
    tmp = _nbytes((k, bn), BF16) + _nbytes((bm, bn), F32)
    single = _nbytes((bm, k), a.dtype)
    sem = ("arbitrary", "arbitrary")
    if next_gain is None:
        return pl.pallas_call(
            functools.partial(_matmul_res_kernel, scale=scale),
            grid=(m // bm, n // bn),
            in_specs=in_specs,
            out_specs=tile,
            out_shape=jax.ShapeDtypeStruct((m, n), F32),
            compiler_params=_params(blk, tmp, sem, single),
            name="matmul_res",
        )(a, w, x)
    blk += _nbytes((1, bn), F32) + _nbytes((bm, bn), BF16) + _nbytes((bm, V7X_LANES), F32)
    return pl.pallas_call(
        functools.partial(_matmul_res_norm_kernel, scale=scale),
        grid=(m // bm, n // bn),
        in_specs=in_specs + [pl.BlockSpec((1, bn), lambda i, j: (0, j))],
        out_specs=[tile, tile, pl.BlockSpec((bm, V7X_LANES), lambda i, j: (i, 0))],
        out_shape=[jax.ShapeDtypeStruct((m, n), F32), jax.ShapeDtypeStruct((m, n), BF16),
                   jax.ShapeDtypeStruct((m, V7X_LANES), F32)],
        compiler_params=_params(blk, tmp, sem, single),
        name="matmul_res_norm",
    )(a, w, x, next_gain.astype(F32).reshape(1, n))


def _glu_kernel(a_ref, wa_ref, wb_ref, o_ref):
    a = a_ref[...].astype(BF16)
    o_ref[...] = (_wdot(a, wa_ref) * jax.nn.sigmoid(_wdot(a, wb_ref))).astype(o_ref.dtype)


def glu_matmul(a, w, layer, bm=2048, bn=256):
    m, k = a.shape
    n = w.shape[2] // 2
    bm = min(bm, m)
    nj = n // bn
    blk = 2 * _nbytes((k, bn), F32) + _nbytes((bm, bn), BF16)
    tmp = _nbytes((bm, k), BF16) + 2 * _nbytes((k, bn), BF16) + 2 * _nbytes((bm, bn), F32)
    return pl.pallas_call(
        _glu_kernel,
        grid=(m // bm, nj),
        in_specs=[_resident((bm, k), lambda i, j: (i, 0)),
                  _weight(k, bn, layer, lambda j: j),
                  _weight(k, bn, layer, lambda j: j + nj)],
        out_specs=pl.BlockSpec((bm, bn), lambda i, j: (i, j)),
        out_shape=jax.ShapeDtypeStruct((m, n), BF16),
        compiler_params=_params(blk, tmp, ("arbitrary", "arbitrary"), _nbytes((bm, k), a.dtype)),
        name="glu_matmul",
    )(a, w, w)


LOG2E = math.log2(math.e)


def _segment_mean_sq(x, seg_ref):
    sq = x * x
    hi = sq.astype(BF16)
    lo = (sq - hi.astype(F32)).astype(BF16)
    s = (jnp.dot(hi, seg_ref[...], preferred_element_type=F32)
         + jnp.dot(lo, seg_ref[...], preferred_element_type=F32))
    return s * (1.0 / HEAD_DIM)


def _attn_kernel(sink_ref, q_ref, kp_ref, kc_ref, vp_ref, vc_ref, bias_ref, qg_ref, kg_ref,
                 seg_ref, o_ref, s_scr, p_scr, sink_scr):
    lanes = V7X_LANES
    low = lax.broadcasted_iota(jnp.int32, (1, lanes), 1) < HEAD_DIM
    high = jnp.logical_not(low)

    kk = jnp.concatenate([kp_ref[...], kc_ref[...]], axis=0)
    vv = jnp.concatenate([vp_ref[...], vc_ref[...]], axis=0)
    ones = jnp.ones((HEAD_DIM, 2 * BLOCK), F32)

    k_low, k_high, v_low, v_high = [], [], [], []
    for c in range(KV_WIDTH // lanes):
        kc = kk[:, c * lanes:(c + 1) * lanes]
        kc = kc * lax.rsqrt(_segment_mean_sq(kc, seg_ref) + EPS) * kg_ref[:, c * lanes:(c + 1) * lanes]
        ks = pltpu.roll(kc, HEAD_DIM, axis=1)
        vt = vv[:, c * lanes:(c + 1) * lanes].T
        for odd in range(2):
            k_low.append(jnp.where(low, ks if odd else kc, 0.0).astype(BF16))
            k_high.append(jnp.where(high, kc if odd else ks, 0.0).astype(BF16))
            vth = vt[odd * HEAD_DIM:(odd + 1) * HEAD_DIM, :]
            v_low.append(jnp.concatenate([vth, ones], axis=0).astype(BF16))
            v_high.append(jnp.concatenate([ones, vth], axis=0).astype(BF16))

    n_pairs = ATTN_WIDTH // lanes
    for pair in range(n_pairs):
        kv = pair // (Q_GROUP // 2)
        qt = q_ref[:, pair * lanes:(pair + 1) * lanes].T
        ms = [jnp.mean(jnp.square(qt[h * HEAD_DIM:(h + 1) * HEAD_DIM]), axis=0, keepdims=True)
              for h in range(2)]
        rs = jnp.concatenate([jnp.broadcast_to(lax.rsqrt(m + EPS), (HEAD_DIM, BLOCK)) for m in ms], axis=0)
        qt = (qt * rs * qg_ref[...]).astype(BF16)
        for half, kh in enumerate((k_low[kv], k_high[kv])):
            head = 2 * pair + half
            s_scr[head] = jnp.dot(kh, qt, preferred_element_type=F32) + bias_ref[head]

    for head in range(N_Q_HEADS):
        s = s_scr[head]
        sink = sink_ref[head]
        mx = jnp.maximum(jnp.max(s, axis=0, keepdims=True), sink)
        p_scr[head] = jnp.exp2(s - mx).astype(BF16)
        sink_scr[head:head + 1, :] = jnp.exp2(sink - mx)

    for pair in range(n_pairs):
        kv = pair // (Q_GROUP // 2)
        num, den = [], []
        for half, vh in enumerate((v_low[kv], v_high[kv])):
            head = 2 * pair + half
            o = jnp.dot(vh, p_scr[head], preferred_element_type=F32)
            lo_rows, hi_rows = o[:HEAD_DIM], o[HEAD_DIM:]
            num.append(lo_rows if half == 0 else hi_rows)
            den.append((hi_rows if half == 0 else lo_rows) + sink_scr[head:head + 1, :])
        ot = jnp.concatenate(num, axis=0) / jnp.concatenate(den, axis=0)
        o_ref[:, pair * lanes:(pair + 1) * lanes] = ot.T.astype(o_ref.dtype)


def attention(proj, bias, q_gain, k_gain, sinks, seq):
    m = proj.shape[0]
    nb = seq // BLOCK
    nblk = m // BLOCK
    lanes = V7X_LANES
    seg_id = jnp.arange(lanes) // HEAD_DIM
    seg = (seg_id[:, None] == seg_id[None, :]).astype(BF16)
    qg = jnp.broadcast_to(jnp.tile(q_gain.astype(F32) * (HEAD_DIM ** -0.5 * LOG2E), 2)[:, None], (lanes, BLOCK))
    kg = jnp.tile(k_gain.astype(F32), N_KV_HEADS).reshape(1, KV_WIDTH)
    kcol, vcol = OFF_K // KV_WIDTH, OFF_V // KV_WIDTH
    table = (None,) + bias.shape[1:]

    def prev(i):
        return jnp.where(i % nb == 0, i, i - 1)

    blk = (_nbytes((BLOCK, ATTN_WIDTH), F32) + 4 * _nbytes((BLOCK, KV_WIDTH), F32)
           + _nbytes(bias.shape[1:], F32) + _nbytes((BLOCK, ATTN_WIDTH), BF16))
    scratch = (_nbytes((N_Q_HEADS, 2 * BLOCK, BLOCK), F32) + _nbytes((N_Q_HEADS, 2 * BLOCK, BLOCK), BF16)
               + _nbytes((N_Q_HEADS, BLOCK), F32))
    return pl.pallas_call(
        _attn_kernel,
        grid=(nblk,),
        in_specs=[pl.BlockSpec(memory_space=pltpu.SMEM),
                  pl.BlockSpec((BLOCK, ATTN_WIDTH), lambda i: (i, 0)),
                  pl.BlockSpec((BLOCK, KV_WIDTH), lambda i: (prev(i), kcol)),
                  pl.BlockSpec((BLOCK, KV_WIDTH), lambda i: (i, kcol)),
                  pl.BlockSpec((BLOCK, KV_WIDTH), lambda i: (prev(i), vcol)),
                  pl.BlockSpec((BLOCK, KV_WIDTH), lambda i: (i, vcol)),
                  pl.BlockSpec(table, lambda i: (jnp.where(i % nb == 0, 0, 1), 0, 0, 0)),
                  pl.BlockSpec((lanes, BLOCK), lambda i: (0, 0)),
                  pl.BlockSpec((1, KV_WIDTH), lambda i: (0, 0)),
                  pl.BlockSpec((lanes, lanes), lambda i: (0, 0))],
        out_specs=pl.BlockSpec((BLOCK, ATTN_WIDTH), lambda i: (i, 0)),
        out_shape=jax.ShapeDtypeStruct((m, ATTN_WIDTH), BF16),
        scratch_shapes=[pltpu.VMEM((N_Q_HEADS, 2 * BLOCK, BLOCK), F32),
                        pltpu.VMEM((N_Q_HEADS, 2 * BLOCK, BLOCK), BF16),
                        pltpu.VMEM((N_Q_HEADS, BLOCK), F32)],
        compiler_params=_params(blk, scratch, semantics=("arbitrary",)),
        name="swa_attention",
    )(sinks.astype(F32) * LOG2E, proj, proj, proj, proj, proj, bias, qg, kg, seg)


def _t5_bucket(dist):
    max_exact = N_BUCKETS // 2
    n = jnp.maximum(dist, 0)
    ratio = jnp.log(jnp.maximum(n, 1).astype(F32) / max_exact) / math.log(MAX_DISTANCE / max_exact)
    large = jnp.minimum(max_exact + (ratio * (N_BUCKETS - max_exact)).astype(jnp.int32), N_BUCKETS - 1)
    return jnp.where(n < max_exact, n, large)


def band_bias(rel_bias):
    keys, period = 2 * BLOCK, 4 * BLOCK
    by_dist = rel_bias.astype(F32)[_t5_bucket(jnp.arange(WINDOW))].T * LOG2E
    v = jnp.concatenate([by_dist, jnp.full((N_Q_HEADS, period - WINDOW), -jnp.inf, F32)], axis=1)
    a = jnp.tile(v, (1, keys))[:, :keys * (period - 1)].reshape(N_Q_HEADS, keys, period - 1)
    rest = a[:, :, BLOCK:2 * BLOCK]
    c = jnp.arange(keys)[None, :, None]
    first = jnp.where(c >= BLOCK, rest, -jnp.inf)
    return jnp.stack([first, rest])


CONV_HALO = 32
CONV_ROWS = 32


def _conv_kernel(a_ref, g_ref, w_ref, b_ref, lg_ref, lb_ref, pw_ref, o_ref, ubuf, shbuf, ybuf, *, nt, tt):
    t = pl.program_id(0) % nt

    @pl.when(t == 0)
    def _():
        ubuf[0:CONV_HALO, :] = jnp.zeros((CONV_HALO, CONV_WIDTH), F32)

    @pl.when(t != 0)
    def _():
        ubuf[0:CONV_HALO, :] = ubuf[tt:tt + CONV_HALO, :]

    ubuf[CONV_HALO:CONV_HALO + tt, :] = a_ref[...] * jax.nn.sigmoid(g_ref[...])

    first = CONV_HALO - (CONV_KERNEL - 1)
    cw = 2 * V7X_LANES

    span = tt + CONV_HALO - V7X_SUBLANES
    for r in range(1, V7X_SUBLANES):
        shbuf[r - 1, 0:span, :] = ubuf[r:r + span, :]

    def rows(rb, carry):
        base = pl.multiple_of(rb * CONV_ROWS, CONV_ROWS)
        for cb in range(CONV_WIDTH // cw):
            cols = slice(cb * cw, (cb + 1) * cw)
            acc = jnp.zeros((CONV_ROWS, cw), F32) + b_ref[:, cols]
            for j in range(CONV_KERNEL):
                r, a = (first + j) % V7X_SUBLANES, (first + j) // V7X_SUBLANES * V7X_SUBLANES
                tap = (ubuf[pl.ds(base + a, CONV_ROWS), cols] if r == 0
                       else shbuf[r - 1, pl.ds(base + a, CONV_ROWS), cols])
                acc = acc + w_ref[j:j + 1, cols] * tap
            ybuf[pl.ds(base, CONV_ROWS), cols] = acc
        return carry

    lax.fori_loop(0, tt // CONV_ROWS, rows, 0)

    y = ybuf[...]
    mu = jnp.mean(y, axis=-1, keepdims=True)
    yc = y - mu
    var = jnp.mean(yc * yc, axis=-1, keepdims=True)
    z = yc * lax.rsqrt(var + EPS) * lg_ref[...] + lb_ref[...]
    z = z * jax.nn.sigmoid(z)
    o_ref[...] = jnp.dot(z.astype(BF16), pw_ref[...], preferred_element_type=F32).astype(o_ref.dtype)


def conv_branch(proj, w_dw, b_dw, ln_g, ln_b, w_pw, seq, tt=512):
    m = proj.shape[0]
    tt = min(tt, seq)
    nt = seq // tt
    acol = OFF_CONV // CONV_WIDTH
    w_pad = jnp.zeros((CONV_HALO, CONV_WIDTH), F32).at[:CONV_KERNEL].set(w_dw.astype(F32))
    row = lambda v: v.astype(F32).reshape(1, CONV_WIDTH)
    blk = (2 * _nbytes((tt, CONV_WIDTH), F32) + _nbytes((CONV_HALO, CONV_WIDTH), F32)
           + 3 * _nbytes((1, CONV_WIDTH), F32) + _nbytes((CONV_WIDTH, CONV_WIDTH), BF16)
           + _nbytes((tt, CONV_WIDTH), BF16))
    scratch = V7X_SUBLANES * _nbytes((tt + CONV_HALO, CONV_WIDTH), F32) + _nbytes((tt, CONV_WIDTH), F32)
    return pl.pallas_call(
        functools.partial(_conv_kernel, nt=nt, tt=tt),
        grid=(m // tt,),
        in_specs=[pl.BlockSpec((tt, CONV_WIDTH), lambda i: (i, acol)),
                  pl.BlockSpec((tt, CONV_WIDTH), lambda i: (i, acol + 1)),
                  pl.BlockSpec((CONV_HALO, CONV_WIDTH), lambda i: (0, 0)),
                  pl.BlockSpec((1, CONV_WIDTH), lambda i: (0, 0)),
                  pl.BlockSpec((1, CONV_WIDTH), lambda i: (0, 0)),
                  pl.BlockSpec((1, CONV_WIDTH), lambda i: (0, 0)),
                  pl.BlockSpec((CONV_WIDTH, CONV_WIDTH), lambda i: (0, 0))],
        out_specs=pl.BlockSpec((tt, CONV_WIDTH), lambda i: (i, 0)),
        out_shape=jax.ShapeDtypeStruct((m, CONV_WIDTH), BF16),
        scratch_shapes=[pltpu.VMEM((tt + CONV_HALO, CONV_WIDTH), F32),
                        pltpu.VMEM((V7X_SUBLANES - 1, tt + CONV_HALO, CONV_WIDTH), F32),
                        pltpu.VMEM((tt, CONV_WIDTH), F32)],
        compiler_params=_params(blk, scratch, semantics=("arbitrary",)),
        name="conv_branch",
    )(proj, proj, w_pad, row(b_dw), row(ln_g), row(ln_b), w_pw)


def _swap_re_im(h):
    lanes = V7X_LANES
    return jnp.concatenate([pltpu.roll(h[:, k * lanes:(k + 1) * lanes], SSM_STATE, axis=1)
                            for k in range(h.shape[1] // lanes)], axis=1)


def _ssm_kernel(u_ref, ws_ref, wi_ref, wxt_ref, a1_ref, a2_ref, d_ref, o_ref, *, nc):
    x = jnp.concatenate([u_ref[pl.ds(t, nc, stride=SSM_CHUNK), :] for t in range(SSM_CHUNK)], axis=1)
    xb = x.astype(BF16)
    h = jnp.dot(xb, ws_ref[0], preferred_element_type=F32)
    row = lax.broadcasted_iota(jnp.int32, (nc, 1), 0)
    level = 0
    while (1 << level) < nc:
        d = 1 << level
        prev = jnp.where(row >= d, pltpu.roll(h, d, axis=0), 0.0)
        h = h + a1_ref[0, level:level + 1, :] * prev + a2_ref[0, level:level + 1, :] * _swap_re_im(prev)
        level += 1
    hp = jnp.where(row >= 1, pltpu.roll(h, 1, axis=0), 0.0).astype(BF16)
    y = (jnp.dot(xb, wi_ref[0], preferred_element_type=F32)
         + lax.dot_general(hp, wxt_ref[0], (((1,), (1,)), ((), ())), preferred_element_type=F32)
         + d_ref[0] * x)
    y = jax.nn.gelu(y)
    for t in range(SSM_CHUNK):
        o_ref[pl.ds(t, nc, stride=SSM_CHUNK), :] = y[:, t * SSM_QCH:(t + 1) * SSM_QCH]


def ssm_branch(proj, operands, d_skip, layer, seq):
    w_state, w_intra, w_inter_t, a1, a2 = operands
    m = proj.shape[0]
    nseq = m // seq
    nc = seq // SSM_CHUNK
    ucol = OFF_SSM // SSM_QCH
    kw = SSM_CHUNK * SSM_QCH
    sw = 2 * SSM_QSTATE
    nlev = a1.shape[2]
    d_t = jnp.tile(d_skip.astype(F32).reshape(-1, SSM_NQ, 1, SSM_QCH), (1, 1, 1, SSM_CHUNK))
    blk = (2 * _nbytes((seq, SSM_QCH), F32) + 2 * _nbytes((kw, sw), BF16) + _nbytes((kw, kw), BF16)
           + 2 * _nbytes((nlev, sw), F32) + _nbytes((1, kw), F32))
    tmp = 6 * _nbytes((nc, kw), F32)
    per_q = lambda shape: pl.BlockSpec((None, 1) + shape, lambda q, b: (layer, q, 0, 0))
    return pl.pallas_call(
        functools.partial(_ssm_kernel, nc=nc),
        grid=(SSM_NQ, nseq),
        in_specs=[pl.BlockSpec((seq, SSM_QCH), lambda q, b: (b, ucol + q)),
                  per_q((kw, sw)), per_q((kw, kw)), per_q((kw, sw)),
                  per_q((nlev, sw)), per_q((nlev, sw)), per_q((1, kw))],
        out_specs=pl.BlockSpec((seq, SSM_QCH), lambda q, b: (b, q)),
        out_shape=jax.ShapeDtypeStruct((m, SSM_WIDTH), F32),
        compiler_params=_params(blk, tmp, semantics=("arbitrary", "arbitrary")),
        name="ssm_scan",
    )(proj, w_state, w_intra, w_inter_t, a1, a2, d_t)


def _cmul(ar, ai, br, bi):
    return ar * br - ai * bi, ar * bi + ai * br


def ssm_operands(a_re, a_im, log_dt, b_re, b_im, c_re, c_im, n_chunks):
    f32 = F32
    dt = jnp.exp(log_dt.astype(f32))[:, None]
    lam_re, lam_im = a_re.astype(f32), a_im.astype(f32)
    mag = jnp.exp(dt * lam_re)
    ang = dt * lam_im
    lb_re, lb_im = mag * jnp.cos(ang), mag * jnp.sin(ang)
    nr = lb_re - 1.0
    den = lam_re * lam_re + lam_im * lam_im
    coef_re = (nr * lam_re + lb_im * lam_im) / den
    coef_im = (lb_im * lam_re - nr * lam_im) / den
    br, bi = b_re.astype(f32), b_im.astype(f32)
    bb_re = coef_re[..., None] * br - coef_im[..., None] * bi
    bb_im = coef_re[..., None] * bi + coef_im[..., None] * br
    cr, ci = c_re.astype(f32), c_im.astype(f32)

    pw = [(jnp.ones_like(lb_re), jnp.zeros_like(lb_im))]
    for _ in range(SSM_CHUNK):
        pw.append(_cmul(pw[-1][0], pw[-1][1], lb_re, lb_im))

    T, Q, C, P = SSM_CHUNK, SSM_QGROUPS, SSM_GROUP, SSM_STATE
    rows_q = Q * C
    pw_re = jnp.stack([p[0] for p in pw])
    pw_im = jnp.stack([p[1] for p in pw])

    def group_diag(blocks):
        w = blocks.shape[-1]
        b = blocks.reshape(T, SSM_NQ, rows_q, w).transpose(1, 0, 2, 3)
        keep = (jnp.arange(rows_q)[:, None] // C) == (jnp.arange(Q * w)[None, :] // w)
        return jnp.where(keep, jnp.concatenate([b] * Q, axis=-1), 0).reshape(SSM_NQ, T * rows_q, Q * w)

    e_re, e_im = _cmul(pw_re[:T, :, :, None], pw_im[:T, :, :, None], bb_re, bb_im)
    e_cp = jnp.concatenate([jnp.swapaxes(e_re, 2, 3), jnp.swapaxes(e_im, 2, 3)], axis=-1)

    w_state = group_diag(e_cp[::-1].astype(BF16))

    f_re, f_im = _cmul(cr, ci, pw_re[1:, :, None, :], pw_im[1:, :, None, :])
    w_inter_t = group_diag(jnp.concatenate([f_re, -f_im], axis=-1).astype(BF16))

    ker = (jnp.einsum('gdp,lgpc->lgcd', cr, e_re, precision=lax.Precision.HIGHEST)
           - jnp.einsum('gdp,lgpc->lgcd', ci, e_im, precision=lax.Precision.HIGHEST))
    kbd = group_diag(ker.astype(BF16)).reshape(SSM_NQ, T, rows_q, rows_q)
    zero = jnp.zeros_like(kbd[:, 0])
    w_intra = jnp.concatenate(
        [jnp.concatenate([kbd[:, t - s] if t >= s else zero for t in range(T)], axis=2) for s in range(T)],
        axis=1)

    a = pw[T]
    a1, a2 = [], []
    level = 0
    while (1 << level) < n_chunks:
        a1.append(jnp.concatenate([a[0], a[0]], axis=-1).reshape(SSM_NQ, Q * 2 * P))
        a2.append(jnp.concatenate([-a[1], a[1]], axis=-1).reshape(SSM_NQ, Q * 2 * P))
        a = _cmul(a[0], a[1], a[0], a[1])
        level += 1
    return w_state, w_intra, w_inter_t, jnp.stack(a1, axis=1), jnp.stack(a2, axis=1)


def _merge_kernel(oa_ref, oc_ref, os_ref, ga_ref, gc_ref, gs_ref, wa_ref, wc_ref, ws_ref, o_ref):
    ya = _wdot(oa_ref[...], wa_ref)
    yc = _wdot(oc_ref[...], wc_ref)
    ys = _wdot(os_ref[...], ws_ref)
    merged = (jax.nn.sigmoid(ga_ref[...]) * ya + jax.nn.sigmoid(gc_ref[...]) * yc
              + jax.nn.sigmoid(gs_ref[...]) * ys)
    o_ref[...] = merged.astype(o_ref.dtype)


def branch_merge(o_attn, o_conv, o_ssm, proj, w_branch, layer, bm=2048, bn=256):
    m = proj.shape[0]
    bm = min(bm, m)
    gcol = OFF_GATE // bn
    gstep = D_MODEL // bn
    blk = 3 * _nbytes((bm, bn), F32) + _nbytes((MIX_WIDTH, bn), F32) + _nbytes((bm, bn), BF16)
    tmp = _nbytes((MIX_WIDTH, bn), BF16) + 4 * _nbytes((bm, bn), F32)

    def wrows(rows, row_tile):
        return pl.BlockSpec((None, rows, bn), lambda i, j: (layer, row_tile, j))

    return pl.pallas_call(
        _merge_kernel,
        grid=(m // bm, D_MODEL // bn),
        in_specs=[_resident((bm, ATTN_WIDTH), lambda i, j: (i, 0)),
                  _resident((bm, CONV_WIDTH), lambda i, j: (i, 0)),
                  _resident((bm, SSM_WIDTH), lambda i, j: (i, 0)),
                  pl.BlockSpec((bm, bn), lambda i, j: (i, gcol + j)),
                  pl.BlockSpec((bm, bn), lambda i, j: (i, gcol + gstep + j)),
                  pl.BlockSpec((bm, bn), lambda i, j: (i, gcol + 2 * gstep + j)),
                  wrows(ATTN_WIDTH, 0),
                  wrows(CONV_WIDTH, ATTN_WIDTH // CONV_WIDTH),
                  wrows(SSM_WIDTH, (ATTN_WIDTH + CONV_WIDTH) // SSM_WIDTH)],
        out_specs=pl.BlockSpec((bm, bn), lambda i, j: (i, j)),
        out_shape=jax.ShapeDtypeStruct((m, D_MODEL), BF16),
        compiler_params=_params(blk, tmp, ("arbitrary", "arbitrary"), _nbytes((bm, MIX_WIDTH), BF16)),
        name="branch_merge",
    )(o_attn, o_conv, o_ssm, proj, proj, proj, w_branch, w_branch, w_branch)


FFN_OUT_ROWS = 1024


def _ffn(x, normed, w_in, w_out, layer, next_gain):
    act = swiglu_in(*normed, w_in, layer)
    return matmul_res(act, w_out, layer, x, 0.5, bm=FFN_OUT_ROWS, next_gain=next_gain)


def _layer(x, normed, seq, bias, ssm_ops, p, layer):
    at = lambda name: p[name][layer]
    x, *normed = _ffn(x, normed, p['w_ffn1_in'], p['w_ffn1_out'], layer, at('mix_norm'))
    proj = matmul(*normed, p['w_in'], layer, F32)
    o_attn = attention(proj, bias, at('q_norm'), at('k_norm'), at('attn_sinks'), seq)
    o_conv = conv_branch(proj, at('conv_dw'), at('conv_dw_bias'), at('conv_ln_g'), at('conv_ln_b'),
                         at('conv_pw').astype(BF16), seq)
    y_ssm = ssm_branch(proj, ssm_ops, p['ssm_d'], layer, seq)
    o_ssm = glu_matmul(y_ssm, p['ssm_glu'], layer)
    merged = branch_merge(o_attn, o_conv, o_ssm, proj, p['w_branch'], layer)
    x, *normed = matmul_res(merged, p['w_out'], layer, x, 1.0, bm=2048, next_gain=at('ffn2_norm'))
    if layer + 1 == DEPTH:
        return _ffn(x, normed, p['w_ffn2_in'], p['w_ffn2_out'], layer, None), None
    x, *normed = _ffn(x, normed, p['w_ffn2_in'], p['w_ffn2_out'], layer, p['ffn1_norm'][layer + 1])
    return x, normed


def kernel(x, rel_bias, ffn1_norm, w_ffn1_in, w_ffn1_out, mix_norm, w_in, q_norm, k_norm, attn_sinks,
           conv_dw, conv_dw_bias, conv_ln_g, conv_ln_b, conv_pw, ssm_a_re, ssm_a_im, ssm_log_dt,
           ssm_b_re, ssm_b_im, ssm_c_re, ssm_c_im, ssm_d, ssm_glu, w_branch, w_out, ffn2_norm,
           w_ffn2_in, w_ffn2_out):
    layer_params = dict(
        ffn1_norm=ffn1_norm, w_ffn1_in=w_ffn1_in, w_ffn1_out=w_ffn1_out, mix_norm=mix_norm, w_in=w_in,
        q_norm=q_norm, k_norm=k_norm, attn_sinks=attn_sinks, conv_dw=conv_dw, conv_dw_bias=conv_dw_bias,
        conv_ln_g=conv_ln_g, conv_ln_b=conv_ln_b, conv_pw=conv_pw, ssm_a_re=ssm_a_re, ssm_a_im=ssm_a_im,
        ssm_log_dt=ssm_log_dt, ssm_b_re=ssm_b_re, ssm_b_im=ssm_b_im, ssm_c_re=ssm_c_re, ssm_c_im=ssm_c_im,
        ssm_d=ssm_d, ssm_glu=ssm_glu, w_branch=w_branch, w_out=w_out, ffn2_norm=ffn2_norm,
        w_ffn2_in=w_ffn2_in, w_ffn2_out=w_ffn2_out)
    b, seq, d = x.shape
    bias = band_bias(rel_bias)
    ssm_ops = jax.vmap(functools.partial(ssm_operands, n_chunks=seq // SSM_CHUNK))(
        ssm_a_re, ssm_a_im, ssm_log_dt, ssm_b_re, ssm_b_im, ssm_c_re, ssm_c_im)
    xf = x.reshape(b * seq, d)
    normed = norm_prep(xf, ffn1_norm[0])
    for layer in range(DEPTH):
        xf, normed = _layer(xf, normed, seq, bias, ssm_ops, layer_params, layer)
    return xf.reshape(b, seq, d)
```

```python
import functools
import math

import jax
import jax.numpy as jnp
from jax import lax
from jax.experimental import pallas as pl
from jax.experimental.pallas import tpu as pltpu

F32 = jnp.float32
BF16 = jnp.bfloat16

D_MODEL = 4096
DEPTH = 2
N_Q_HEADS = 32
N_KV_HEADS = 8
HEAD_DIM = 64
Q_GROUP = N_Q_HEADS // N_KV_HEADS
ATTN_WIDTH = N_Q_HEADS * HEAD_DIM
KV_WIDTH = N_KV_HEADS * HEAD_DIM
WINDOW = 128
BLOCK = 128
N_BUCKETS = 32
MAX_DISTANCE = 128
CONV_WIDTH = D_MODEL // 4
CONV_KERNEL = 31
SSM_WIDTH = D_MODEL // 4
SSM_GROUP = 16
SSM_GROUPS = SSM_WIDTH // SSM_GROUP
SSM_STATE = 64
D_FF = 256 * ((8 * D_MODEL // 3 + 255) // 256)
N_BRANCH = 3
MIX_WIDTH = ATTN_WIDTH + CONV_WIDTH + SSM_WIDTH
IN_WIDTH = ATTN_WIDTH + 2 * KV_WIDTH + 2 * CONV_WIDTH + SSM_WIDTH + N_BRANCH * D_MODEL
EPS = 1e-6

OFF_K = ATTN_WIDTH
OFF_V = OFF_K + KV_WIDTH
OFF_CONV = OFF_V + KV_WIDTH
OFF_SSM = OFF_CONV + 2 * CONV_WIDTH
OFF_GATE = OFF_SSM + SSM_WIDTH

V7X_LANES = 128
V7X_SUBLANES = 8
V7X_VMEM_BYTES = 64 * 1024 * 1024
V7X_VMEM_INTERNAL_BYTES = 12 * 1024 * 1024

SSM_CHUNK = 8
SSM_QGROUPS = 8
SSM_QCH = SSM_QGROUPS * SSM_GROUP
SSM_NQ = SSM_GROUPS // SSM_QGROUPS
SSM_QSTATE = SSM_QGROUPS * SSM_STATE


def _nbytes(shape, dtype):
    return math.prod(shape) * jnp.dtype(dtype).itemsize


def _params(block_bytes, scratch_bytes=0, semantics=None, single_bytes=0):
    limit = 2 * block_bytes + single_bytes + scratch_bytes + V7X_VMEM_INTERNAL_BYTES
    limit = min(limit, V7X_VMEM_BYTES - 4 * 1024 * 1024)
    return pltpu.CompilerParams(dimension_semantics=semantics, vmem_limit_bytes=int(limit))


def _resident(shape, index_map):
    return pl.BlockSpec(shape, index_map, pipeline_mode=pl.Buffered(1))


def _weight(k, bn, layer, col):
    return pl.BlockSpec((None, k, bn), lambda i, j: (layer, 0, col(j)))


def _wdot(a, w_ref):
    return jnp.dot(a, w_ref[...].astype(BF16), preferred_element_type=F32)


ROW_SUB = 256


def _row_blocks(bm):
    sub = min(ROW_SUB, bm)
    return [slice(r, r + sub) for r in range(0, bm, sub)]


def _lane_partial_sq(y):
    sq = y * y
    part = sq[:, :V7X_LANES]
    for c in range(1, y.shape[1] // V7X_LANES):
        part = part + sq[:, c * V7X_LANES:(c + 1) * V7X_LANES]
    return part


def _row_scale(ssq_ref, d):
    return lax.rsqrt(jnp.sum(ssq_ref[...], axis=-1, keepdims=True) * (1.0 / d) + EPS)


def _norm_prep_kernel(x_ref, g_ref, xg_ref, ssq_ref):
    x = x_ref[...]
    xg_ref[...] = (x * g_ref[...]).astype(xg_ref.dtype)
    ssq_ref[...] = _lane_partial_sq(x)


def norm_prep(x, g, bm=256):
    m, d = x.shape
    blk = (_nbytes((bm, d), F32) + _nbytes((bm, d), BF16) + _nbytes((1, d), F32)
           + _nbytes((bm, V7X_LANES), F32))
    return pl.pallas_call(
        _norm_prep_kernel,
        grid=(m // bm,),
        in_specs=[pl.BlockSpec((bm, d), lambda i: (i, 0)),
                  pl.BlockSpec((1, d), lambda i: (0, 0))],
        out_specs=[pl.BlockSpec((bm, d), lambda i: (i, 0)),
                   pl.BlockSpec((bm, V7X_LANES), lambda i: (i, 0))],
        out_shape=[jax.ShapeDtypeStruct((m, d), BF16),
                   jax.ShapeDtypeStruct((m, V7X_LANES), F32)],
        compiler_params=_params(blk, semantics=("arbitrary",)),
        name="norm_prep",
    )(x, g.reshape(1, d))


def _swiglu_kernel(xg_ref, ssq_ref, wg_ref, wu_ref, o_ref, r_scr):
    @pl.when(pl.program_id(1) == 0)
    def _():
        r_scr[...] = _row_scale(ssq_ref, xg_ref.shape[1])

    wg = wg_ref[...].astype(BF16)
    wu = wu_ref[...].astype(BF16)
    for rows in _row_blocks(xg_ref.shape[0]):
        h = xg_ref[rows, :]
        r = r_scr[rows, :]
        g = r * jnp.dot(h, wg, preferred_element_type=F32)
        u = r * jnp.dot(h, wu, preferred_element_type=F32)
        o_ref[rows, :] = (g * jax.nn.sigmoid(g) * u).astype(o_ref.dtype)


def swiglu_in(xg, ssq, w, layer, bm=2048, bn=256):
    m, k = xg.shape
    f = w.shape[2] // 2
    bm = min(bm, m)
    nj = f // bn
    blk = 2 * _nbytes((k, bn), F32) + _nbytes((bm, bn), BF16)
    tmp = 2 * _nbytes((k, bn), BF16) + 2 * _nbytes((bm, bn), F32) + _nbytes((bm, V7X_LANES), F32)
    single = _nbytes((bm, k), BF16) + _nbytes((bm, V7X_LANES), F32)
    return pl.pallas_call(
        _swiglu_kernel,
        grid=(m // bm, nj),
        in_specs=[_resident((bm, k), lambda i, j: (i, 0)),
                  _resident((bm, V7X_LANES), lambda i, j: (i, 0)),
                  _weight(k, bn, layer, lambda j: j),
                  _weight(k, bn, layer, lambda j: j + nj)],
        out_specs=pl.BlockSpec((bm, bn), lambda i, j: (i, j)),
        out_shape=jax.ShapeDtypeStruct((m, f), BF16),
        scratch_shapes=[pltpu.VMEM((bm, 1), F32)],
        compiler_params=_params(blk, tmp, ("arbitrary", "arbitrary"), single),
        name="swiglu_in",
    )(xg, ssq, w, w)


def _matmul_kernel(xg_ref, ssq_ref, w_ref, o_ref, r_scr):
    @pl.when(pl.program_id(1) == 0)
    def _():
        r_scr[...] = _row_scale(ssq_ref, xg_ref.shape[1])

    w = w_ref[...].astype(BF16)
    for rows in _row_blocks(xg_ref.shape[0]):
        y = r_scr[rows, :] * jnp.dot(xg_ref[rows, :], w, preferred_element_type=F32)
        o_ref[rows, :] = y.astype(o_ref.dtype)


def matmul(xg, ssq, w, layer, out_dtype, bm=2048, bn=512):
    m, k = xg.shape
    n = w.shape[2]
    bm, bn = min(bm, m), min(bn, n)
    blk = _nbytes((k, bn), F32) + _nbytes((bm, bn), out_dtype)
    tmp = _nbytes((k, bn), BF16) + _nbytes((bm, bn), F32) + _nbytes((bm, V7X_LANES), F32)
    single = _nbytes((bm, k), BF16) + _nbytes((bm, V7X_LANES), F32)
    return pl.pallas_call(
        _matmul_kernel,
        grid=(m // bm, n // bn),
        in_specs=[_resident((bm, k), lambda i, j: (i, 0)),
                  _resident((bm, V7X_LANES), lambda i, j: (i, 0)),
                  _weight(k, bn, layer, lambda j: j)],
        out_specs=pl.BlockSpec((bm, bn), lambda i, j: (i, j)),
        out_shape=jax.ShapeDtypeStruct((m, n), out_dtype),
        scratch_shapes=[pltpu.VMEM((bm, 1), F32)],
        compiler_params=_params(blk, tmp, ("arbitrary", "arbitrary"), single),
        name="matmul",
    )(xg, ssq, w)


def _matmul_res_kernel(a_ref, w_ref, x_ref, o_ref, *, scale):
    w = w_ref[...].astype(BF16)
    for rows in _row_blocks(a_ref.shape[0]):
        o_ref[rows, :] = x_ref[rows, :] + scale * jnp.dot(a_ref[rows, :], w, preferred_element_type=F32)


def _matmul_res_norm_kernel(a_ref, w_ref, x_ref, g_ref, o_ref, xg_ref, ssq_ref, *, scale):
    @pl.when(pl.program_id(1) == 0)
    def _():
        ssq_ref[...] = jnp.zeros(ssq_ref.shape, F32)

    w = w_ref[...].astype(BF16)
    for rows in _row_blocks(a_ref.shape[0]):
        y = x_ref[rows, :] + scale * jnp.dot(a_ref[rows, :], w, preferred_element_type=F32)
        o_ref[rows, :] = y
        xg_ref[rows, :] = (y * g_ref[...]).astype(xg_ref.dtype)
        ssq_ref[rows, :] += _lane_partial_sq(y)


def matmul_res(a, w, layer, x, scale, bm, next_gain=None, bn=256):
    m, k = a.shape
    n = w.shape[2]
    bm, bn = min(bm, m), min(bn, n)
    tile = pl.BlockSpec((bm, bn), lambda i, j: (i, j))
    in_specs = [_resident((bm, k), lambda i, j: (i, 0)), _weight(k, bn, layer, lambda j: j), tile]
    blk = _nbytes((k, bn), F32) + 2 * _nbytes((bm, bn), F32)
    tmp = _nbytes((k, bn), BF16) + _nbytes((bm, bn), F32)
    single = _nbytes((bm, k), a.dtype)
    sem = ("arbitrary", "arbitrary")
    if next_gain is None:
        return pl.pallas_call(
            functools.partial(_matmul_res_kernel, scale=scale),
            grid=(m // bm, n // bn),
            in_specs=in_specs,
            out_specs=tile,
            out_shape=jax.ShapeDtypeStruct((m, n), F32),
            compiler_params=_params(blk, tmp, sem, single),
            name="matmul_res",
        )(a, w, x)
    blk += _nbytes((1, bn), F32) + _nbytes((bm, bn), BF16) + _nbytes((bm, V7X_LANES), F32)
    return pl.pallas_call(
        functools.partial(_matmul_res_norm_kernel, scale=scale),
        grid=(m // bm, n // bn),
        in_specs=in_specs + [pl.BlockSpec((1, bn), lambda i, j: (0, j))],
        out_specs=[tile, tile, pl.BlockSpec((bm, V7X_LANES), lambda i, j: (i, 0))],
        out_shape=[jax.ShapeDtypeStruct((m, n), F32), jax.ShapeDtypeStruct((m, n), BF16),
                   jax.ShapeDtypeStruct((m, V7X_LANES), F32)],
        compiler_params=_params(blk, tmp, sem, single),
        name="matmul_res_norm",
    )(a, w, x, next_gain.astype(F32).reshape(1, n))


def _glu_kernel(a_ref, wa_ref, wb_ref, o_ref):
    wa = wa_ref[...].astype(BF16)
    wb = wb_ref[...].astype(BF16)
    for rows in _row_blocks(a_ref.shape[0]):
        a = a_ref[rows, :].astype(BF16)
        za = jnp.dot(a, wa, preferred_element_type=F32)
        zb = jnp.dot(a, wb, preferred_element_type=F32)
        o_ref[rows, :] = (za * jax.nn.sigmoid(zb)).astype(o_ref.dtype)


def glu_matmul(a, w, layer, bm=2048, bn=256):
    m, k = a.shape
    n = w.shape[2] // 2
    bm = min(bm, m)
    nj = n // bn
    blk = 2 * _nbytes((k, bn), F32) + _nbytes((bm, bn), BF16)
    tmp = _nbytes((bm, k), BF16) + 2 * _nbytes((k, bn), BF16) + 2 * _nbytes((bm, bn), F32)
    return pl.pallas_call(
        _glu_kernel,
        grid=(m // bm, nj),
        in_specs=[_resident((bm, k), lambda i, j: (i, 0)),
                  _weight(k, bn, layer, lambda j: j),
                  _weight(k, bn, layer, lambda j: j + nj)],
        out_specs=pl.BlockSpec((bm, bn), lambda i, j: (i, j)),
        out_shape=jax.ShapeDtypeStruct((m, n), BF16),
        compiler_params=_params(blk, tmp, ("arbitrary", "arbitrary"), _nbytes((bm, k), a.dtype)),
        name="glu_matmul",
    )(a, w, w)


LOG2E = math.log2(math.e)


def _segment_mean_sq(x, seg_ref):
    sq = x * x
    hi = sq.astype(BF16)
    lo = (sq - hi.astype(F32)).astype(BF16)
    s = (jnp.dot(hi, seg_ref[...], preferred_element_type=F32)
         + jnp.dot(lo, seg_ref[...], preferred_element_type=F32))
    return s * (1.0 / HEAD_DIM)


def _attn_kernel(sink_ref, q_ref, kp_ref, kc_ref, vp_ref, vc_ref, bias_ref, qg_ref, kg_ref,
                 seg_ref, o_ref, s_scr, p_scr, sink_scr):
    lanes = V7X_LANES
    low = lax.broadcasted_iota(jnp.int32, (1, lanes), 1) < HEAD_DIM
    high = jnp.logical_not(low)

    kk = jnp.concatenate([kp_ref[...], kc_ref[...]], axis=0)
    vv = jnp.concatenate([vp_ref[...], vc_ref[...]], axis=0)
    ones = jnp.ones((HEAD_DIM, 2 * BLOCK), F32)

    k_low, k_high, v_low, v_high = [], [], [], []
    for c in range(KV_WIDTH // lanes):
        kc = kk[:, c * lanes:(c + 1) * lanes]
        kc = kc * lax.rsqrt(_segment_mean_sq(kc, seg_ref) + EPS) * kg_ref[:, c * lanes:(c + 1) * lanes]
        ks = pltpu.roll(kc, HEAD_DIM, axis=1)
        vt = vv[:, c * lanes:(c + 1) * lanes].T
        for odd in range(2):
            k_low.append(jnp.where(low, ks if odd else kc, 0.0).astype(BF16))
            k_high.append(jnp.where(high, kc if odd else ks, 0.0).astype(BF16))
            vth = vt[odd * HEAD_DIM:(odd + 1) * HEAD_DIM, :]
            v_low.append(jnp.concatenate([vth, ones], axis=0).astype(BF16))
            v_high.append(jnp.concatenate([ones, vth], axis=0).astype(BF16))

    n_pairs = ATTN_WIDTH // lanes
    for pair in range(n_pairs):
        kv = pair // (Q_GROUP // 2)
        qt = q_ref[:, pair * lanes:(pair + 1) * lanes].T
        ms = [jnp.mean(jnp.square(qt[h * HEAD_DIM:(h + 1) * HEAD_DIM]), axis=0, keepdims=True)
              for h in range(2)]
        rs = jnp.concatenate([jnp.broadcast_to(lax.rsqrt(m + EPS), (HEAD_DIM, BLOCK)) for m in ms], axis=0)
        qt = (qt * rs * qg_ref[...]).astype(BF16)
        for half, kh in enumerate((k_low[kv], k_high[kv])):
            head = 2 * pair + half
            s_scr[head] = jnp.dot(kh, qt, preferred_element_type=F32) + bias_ref[head]

    for head in range(N_Q_HEADS):
        s = s_scr[head]
        sink = sink_ref[head]
        mx = jnp.maximum(jnp.max(s, axis=0, keepdims=True), sink)
        p_scr[head] = jnp.exp2(s - mx).astype(BF16)
        sink_scr[head:head + 1, :] = jnp.exp2(sink - mx)

    for pair in range(n_pairs):
        kv = pair // (Q_GROUP // 2)
        num, den = [], []
        for half, vh in enumerate((v_low[kv], v_high[kv])):
            head = 2 * pair + half
            o = jnp.dot(vh, p_scr[head], preferred_element_type=F32)
            lo_rows, hi_rows = o[:HEAD_DIM], o[HEAD_DIM:]
            num.append(lo_rows if half == 0 else hi_rows)
            den.append((hi_rows if half == 0 else lo_rows) + sink_scr[head:head + 1, :])
        ot = jnp.concatenate(num, axis=0) / jnp.concatenate(den, axis=0)
        o_ref[:, pair * lanes:(pair + 1) * lanes] = ot.T.astype(o_ref.dtype)


def attention(proj, bias, q_gain, k_gain, sinks, seq):
    m = proj.shape[0]
    nb = seq // BLOCK
    nblk = m // BLOCK
    lanes = V7X_LANES
    seg_id = jnp.arange(lanes) // HEAD_DIM
    seg = (seg_id[:, None] == seg_id[None, :]).astype(BF16)
    qg = jnp.broadcast_to(jnp.tile(q_gain.astype(F32) * (HEAD_DIM ** -0.5 * LOG2E), 2)[:, None], (lanes, BLOCK))
    kg = jnp.tile(k_gain.astype(F32), N_KV_HEADS).reshape(1, KV_WIDTH)
    kcol, vcol = OFF_K // KV_WIDTH, OFF_V // KV_WIDTH
    table = (None,) + bias.shape[1:]

    def prev(i):
        return jnp.where(i % nb == 0, i, i - 1)

    blk = (_nbytes((BLOCK, ATTN_WIDTH), F32) + 4 * _nbytes((BLOCK, KV_WIDTH), F32)
           + _nbytes(bias.shape[1:], F32) + _nbytes((BLOCK, ATTN_WIDTH), BF16))
    scratch = (_nbytes((N_Q_HEADS, 2 * BLOCK, BLOCK), F32) + _nbytes((N_Q_HEADS, 2 * BLOCK, BLOCK), BF16)
               + _nbytes((N_Q_HEADS, BLOCK), F32))
    return pl.pallas_call(
        _attn_kernel,
        grid=(nblk,),
        in_specs=[pl.BlockSpec(memory_space=pltpu.SMEM),
                  pl.BlockSpec((BLOCK, ATTN_WIDTH), lambda i: (i, 0)),
                  pl.BlockSpec((BLOCK, KV_WIDTH), lambda i: (prev(i), kcol)),
                  pl.BlockSpec((BLOCK, KV_WIDTH), lambda i: (i, kcol)),
                  pl.BlockSpec((BLOCK, KV_WIDTH), lambda i: (prev(i), vcol)),
                  pl.BlockSpec((BLOCK, KV_WIDTH), lambda i: (i, vcol)),
                  pl.BlockSpec(table, lambda i: (jnp.where(i % nb == 0, 0, 1), 0, 0, 0)),
                  pl.BlockSpec((lanes, BLOCK), lambda i: (0, 0)),
                  pl.BlockSpec((1, KV_WIDTH), lambda i: (0, 0)),
                  pl.BlockSpec((lanes, lanes), lambda i: (0, 0))],
        out_specs=pl.BlockSpec((BLOCK, ATTN_WIDTH), lambda i: (i, 0)),
        out_shape=jax.ShapeDtypeStruct((m, ATTN_WIDTH), BF16),
        scratch_shapes=[pltpu.VMEM((N_Q_HEADS, 2 * BLOCK, BLOCK), F32),
                        pltpu.VMEM((N_Q_HEADS, 2 * BLOCK, BLOCK), BF16),
                        pltpu.VMEM((N_Q_HEADS, BLOCK), F32)],
        compiler_params=_params(blk, scratch, semantics=("arbitrary",)),
        name="swa_attention",
    )(sinks.astype(F32) * LOG2E, proj, proj, proj, proj, proj, bias, qg, kg, seg)


def _t5_bucket(dist):
    max_exact = N_BUCKETS // 2
    n = jnp.maximum(dist, 0)
    ratio = jnp.log(jnp.maximum(n, 1).astype(F32) / max_exact) / math.log(MAX_DISTANCE / max_exact)
    large = jnp.minimum(max_exact + (ratio * (N_BUCKETS - max_exact)).astype(jnp.int32), N_BUCKETS - 1)
    return jnp.where(n < max_exact, n, large)


def band_bias(rel_bias):
    keys, period = 2 * BLOCK, 4 * BLOCK
    by_dist = rel_bias.astype(F32)[_t5_bucket(jnp.arange(WINDOW))].T * LOG2E
    v = jnp.concatenate([by_dist, jnp.full((N_Q_HEADS, period - WINDOW), -jnp.inf, F32)], axis=1)
    a = jnp.tile(v, (1, keys))[:, :keys * (period - 1)].reshape(N_Q_HEADS, keys, period - 1)
    rest = a[:, :, BLOCK:2 * BLOCK]
    c = jnp.arange(keys)[None, :, None]
    first = jnp.where(c >= BLOCK, rest, -jnp.inf)
    return jnp.stack([first, rest])


CONV_HALO = 32
CONV_ROWS = 32


def _conv_kernel(a_ref, g_ref, w_ref, b_ref, lg_ref, lb_ref, pw_ref, o_ref, ubuf, shbuf, ybuf, *, nt, tt):
    t = pl.program_id(0) % nt

    @pl.when(t == 0)
    def _():
        ubuf[0:CONV_HALO, :] = jnp.zeros((CONV_HALO, CONV_WIDTH), F32)

    @pl.when(t != 0)
    def _():
        ubuf[0:CONV_HALO, :] = ubuf[tt:tt + CONV_HALO, :]

    ubuf[CONV_HALO:CONV_HALO + tt, :] = a_ref[...] * jax.nn.sigmoid(g_ref[...])

    first = CONV_HALO - (CONV_KERNEL - 1)
    cw = 2 * V7X_LANES

    span = tt + CONV_HALO - V7X_SUBLANES
    for r in range(1, V7X_SUBLANES):
        shbuf[r - 1, 0:span, :] = ubuf[r:r + span, :]

    def rows(rb, carry):
        base = pl.multiple_of(rb * CONV_ROWS, CONV_ROWS)
        for cb in range(CONV_WIDTH // cw):
            cols = slice(cb * cw, (cb + 1) * cw)
            acc = jnp.zeros((CONV_ROWS, cw), F32) + b_ref[:, cols]
            for j in range(CONV_KERNEL):
                r, a = (first + j) % V7X_SUBLANES, (first + j) // V7X_SUBLANES * V7X_SUBLANES
                tap = (ubuf[pl.ds(base + a, CONV_ROWS), cols] if r == 0
                       else shbuf[r - 1, pl.ds(base + a, CONV_ROWS), cols])
                acc = acc + w_ref[j:j + 1, cols] * tap
            ybuf[pl.ds(base, CONV_ROWS), cols] = acc
        return carry

    lax.fori_loop(0, tt // CONV_ROWS, rows, 0)

    y = ybuf[...]
    mu = jnp.mean(y, axis=-1, keepdims=True)
    yc = y - mu
    var = jnp.mean(yc * yc, axis=-1, keepdims=True)
    z = yc * lax.rsqrt(var + EPS) * lg_ref[...] + lb_ref[...]
    z = z * jax.nn.sigmoid(z)
    o_ref[...] = jnp.dot(z.astype(BF16), pw_ref[...], preferred_element_type=F32).astype(o_ref.dtype)


def conv_branch(proj, w_dw, b_dw, ln_g, ln_b, w_pw, seq, tt=512):
    m = proj.shape[0]
    tt = min(tt, seq)
    nt = seq // tt
    acol = OFF_CONV // CONV_WIDTH
    w_pad = jnp.zeros((CONV_HALO, CONV_WIDTH), F32).at[:CONV_KERNEL].set(w_dw.astype(F32))
    row = lambda v: v.astype(F32).reshape(1, CONV_WIDTH)
    blk = (2 * _nbytes((tt, CONV_WIDTH), F32) + _nbytes((CONV_HALO, CONV_WIDTH), F32)
           + 3 * _nbytes((1, CONV_WIDTH), F32) + _nbytes((CONV_WIDTH, CONV_WIDTH), BF16)
           + _nbytes((tt, CONV_WIDTH), BF16))
    scratch = V7X_SUBLANES * _nbytes((tt + CONV_HALO, CONV_WIDTH), F32) + _nbytes((tt, CONV_WIDTH), F32)
    return pl.pallas_call(
        functools.partial(_conv_kernel, nt=nt, tt=tt),
        grid=(m // tt,),
        in_specs=[pl.BlockSpec((tt, CONV_WIDTH), lambda i: (i, acol)),
                  pl.BlockSpec((tt, CONV_WIDTH), lambda i: (i, acol + 1)),
                  pl.BlockSpec((CONV_HALO, CONV_WIDTH), lambda i: (0, 0)),
                  pl.BlockSpec((1, CONV_WIDTH), lambda i: (0, 0)),
                  pl.BlockSpec((1, CONV_WIDTH), lambda i: (0, 0)),
                  pl.BlockSpec((1, CONV_WIDTH), lambda i: (0, 0)),
                  pl.BlockSpec((CONV_WIDTH, CONV_WIDTH), lambda i: (0, 0))],
        out_specs=pl.BlockSpec((tt, CONV_WIDTH), lambda i: (i, 0)),
        out_shape=jax.ShapeDtypeStruct((m, CONV_WIDTH), BF16),
        scratch_shapes=[pltpu.VMEM((tt + CONV_HALO, CONV_WIDTH), F32),
                        pltpu.VMEM((V7X_SUBLANES - 1, tt + CONV_HALO, CONV_WIDTH), F32),
                        pltpu.VMEM((tt, CONV_WIDTH), F32)],
        compiler_params=_params(blk, scratch, semantics=("arbitrary",)),
        name="conv_branch",
    )(proj, proj, w_pad, row(b_dw), row(ln_g), row(ln_b), w_pw)


def _swap_re_im(h):
    lanes = V7X_LANES
    return jnp.concatenate([pltpu.roll(h[:, k * lanes:(k + 1) * lanes], SSM_STATE, axis=1)
                            for k in range(h.shape[1] // lanes)], axis=1)


def _ssm_kernel(u_ref, ws_ref, wi_ref, wxt_ref, a1_ref, a2_ref, d_ref, o_ref, *, nc):
    x = jnp.concatenate([u_ref[pl.ds(t, nc, stride=SSM_CHUNK), :] for t in range(SSM_CHUNK)], axis=1)
    xb = x.astype(BF16)
    h = jnp.dot(xb, ws_ref[0], preferred_element_type=F32)
    row = lax.broadcasted_iota(jnp.int32, (nc, 1), 0)
    level = 0
    while (1 << level) < nc:
        d = 1 << level
        prev = jnp.where(row >= d, pltpu.roll(h, d, axis=0), 0.0)
        h = h + a1_ref[0, level:level + 1, :] * prev + a2_ref[0, level:level + 1, :] * _swap_re_im(prev)
        level += 1
    hp = jnp.where(row >= 1, pltpu.roll(h, 1, axis=0), 0.0).astype(BF16)
    y = (jnp.dot(xb, wi_ref[0], preferred_element_type=F32)
         + lax.dot_general(hp, wxt_ref[0], (((1,), (1,)), ((), ())), preferred_element_type=F32)
         + d_ref[0] * x)
    y = jax.nn.gelu(y)
    for t in range(SSM_CHUNK):
        o_ref[pl.ds(t, nc, stride=SSM_CHUNK), :] = y[:, t * SSM_QCH:(t + 1) * SSM_QCH]


def ssm_branch(proj, operands, d_skip, layer, seq):
    w_state, w_intra, w_inter_t, a1, a2 = operands
    m = proj.shape[0]
    nseq = m // seq
    nc = seq // SSM_CHUNK
    ucol = OFF_SSM // SSM_QCH
    kw = SSM_CHUNK * SSM_QCH
    sw = 2 * SSM_QSTATE
    nlev = a1.shape[2]
    d_t = jnp.tile(d_skip.astype(F32).reshape(-1, SSM_NQ, 1, SSM_QCH), (1, 1, 1, SSM_CHUNK))
    blk = (2 * _nbytes((seq, SSM_QCH), F32) + 2 * _nbytes((kw, sw), BF16) + _nbytes((kw, kw), BF16)
           + 2 * _nbytes((nlev, sw), F32) + _nbytes((1, kw), F32))
    tmp = 6 * _nbytes((nc, kw), F32)
    per_q = lambda shape: pl.BlockSpec((None, 1) + shape, lambda q, b: (layer, q, 0, 0))
    return pl.pallas_call(
        functools.partial(_ssm_kernel, nc=nc),
        grid=(SSM_NQ, nseq),
        in_specs=[pl.BlockSpec((seq, SSM_QCH), lambda q, b: (b, ucol + q)),
                  per_q((kw, sw)), per_q((kw, kw)), per_q((kw, sw)),
                  per_q((nlev, sw)), per_q((nlev, sw)), per_q((1, kw))],
        out_specs=pl.BlockSpec((seq, SSM_QCH), lambda q, b: (b, q)),
        out_shape=jax.ShapeDtypeStruct((m, SSM_WIDTH), F32),
        compiler_params=_params(blk, tmp, semantics=("arbitrary", "arbitrary")),
        name="ssm_scan",
    )(proj, w_state, w_intra, w_inter_t, a1, a2, d_t)


def _cmul(ar, ai, br, bi):
    return ar * br - ai * bi, ar * bi + ai * br


def ssm_operands(a_re, a_im, log_dt, b_re, b_im, c_re, c_im, n_chunks):
    f32 = F32
    dt = jnp.exp(log_dt.astype(f32))[:, None]
    lam_re, lam_im = a_re.astype(f32), a_im.astype(f32)
    mag = jnp.exp(dt * lam_re)
    ang = dt * lam_im
    lb_re, lb_im = mag * jnp.cos(ang), mag * jnp.sin(ang)
    nr = lb_re - 1.0
    den = lam_re * lam_re + lam_im * lam_im
    coef_re = (nr * lam_re + lb_im * lam_im) / den
    coef_im = (lb_im * lam_re - nr * lam_im) / den
    br, bi = b_re.astype(f32), b_im.astype(f32)
    bb_re = coef_re[..., None] * br - coef_im[..., None] * bi
    bb_im = coef_re[..., None] * bi + coef_im[..., None] * br
    cr, ci = c_re.astype(f32), c_im.astype(f32)

    pw = [(jnp.ones_like(lb_re), jnp.zeros_like(lb_im))]
    for _ in range(SSM_CHUNK):
        pw.append(_cmul(pw[-1][0], pw[-1][1], lb_re, lb_im))

    T, Q, C, P = SSM_CHUNK, SSM_QGROUPS, SSM_GROUP, SSM_STATE
    rows_q = Q * C
    pw_re = jnp.stack([p[0] for p in pw])
    pw_im = jnp.stack([p[1] for p in pw])

    def group_diag(blocks):
        w = blocks.shape[-1]
        b = blocks.reshape(T, SSM_NQ, rows_q, w).transpose(1, 0, 2, 3)
        keep = (jnp.arange(rows_q)[:, None] // C) == (jnp.arange(Q * w)[None, :] // w)
        return jnp.where(keep, jnp.concatenate([b] * Q, axis=-1), 0).reshape(SSM_NQ, T * rows_q, Q * w)

    e_re, e_im = _cmul(pw_re[:T, :, :, None], pw_im[:T, :, :, None], bb_re, bb_im)
    e_cp = jnp.concatenate([jnp.swapaxes(e_re, 2, 3), jnp.swapaxes(e_im, 2, 3)], axis=-1)

    w_state = group_diag(e_cp[::-1].astype(BF16))

    f_re, f_im = _cmul(cr, ci, pw_re[1:, :, None, :], pw_im[1:, :, None, :])
    w_inter_t = group_diag(jnp.concatenate([f_re, -f_im], axis=-1).astype(BF16))

    ker = (jnp.einsum('gdp,lgpc->lgcd', cr, e_re, precision=lax.Precision.HIGHEST)
           - jnp.einsum('gdp,lgpc->lgcd', ci, e_im, precision=lax.Precision.HIGHEST))
    kbd = group_diag(ker.astype(BF16)).reshape(SSM_NQ, T, rows_q, rows_q)
    zero = jnp.zeros_like(kbd[:, 0])
    w_intra = jnp.concatenate(
        [jnp.concatenate([kbd[:, t - s] if t >= s else zero for t in range(T)], axis=2) for s in range(T)],
        axis=1)

    a = pw[T]
    a1, a2 = [], []
    level = 0
    while (1 << level) < n_chunks:
        a1.append(jnp.concatenate([a[0], a[0]], axis=-1).reshape(SSM_NQ, Q * 2 * P))
        a2.append(jnp.concatenate([-a[1], a[1]], axis=-1).reshape(SSM_NQ, Q * 2 * P))
        a = _cmul(a[0], a[1], a[0], a[1])
        level += 1
    return w_state, w_intra, w_inter_t, jnp.stack(a1, axis=1), jnp.stack(a2, axis=1)


def _merge_kernel(oa_ref, oc_ref, os_ref, ga_ref, gc_ref, gs_ref, wa_ref, wc_ref, ws_ref, o_ref):
    wa, wc, ws = (w[...].astype(BF16) for w in (wa_ref, wc_ref, ws_ref))
    for rows in _row_blocks(o_ref.shape[0]):
        ya = jnp.dot(oa_ref[rows, :], wa, preferred_element_type=F32)
        yc = jnp.dot(oc_ref[rows, :], wc, preferred_element_type=F32)
        ys = jnp.dot(os_ref[rows, :], ws, preferred_element_type=F32)
        merged = (jax.nn.sigmoid(ga_ref[rows, :]) * ya + jax.nn.sigmoid(gc_ref[rows, :]) * yc
                  + jax.nn.sigmoid(gs_ref[rows, :]) * ys)
        o_ref[rows, :] = merged.astype(o_ref.dtype)


def branch_merge(o_attn, o_conv, o_ssm, proj, w_branch, layer, bm=2048, bn=256):
    m = proj.shape[0]
    bm = min(bm, m)
    gcol = OFF_GATE // bn
    gstep = D_MODEL // bn
    blk = 3 * _nbytes((bm, bn), F32) + _nbytes((MIX_WIDTH, bn), F32) + _nbytes((bm, bn), BF16)
    tmp = _nbytes((MIX_WIDTH, bn), BF16) + 4 * _nbytes((bm, bn), F32)

    def wrows(rows, row_tile):
        return pl.BlockSpec((None, rows, bn), lambda i, j: (layer, row_tile, j))

    return pl.pallas_call(
        _merge_kernel,
        grid=(m // bm, D_MODEL // bn),
        in_specs=[_resident((bm, ATTN_WIDTH), lambda i, j: (i, 0)),
                  _resident((bm, CONV_WIDTH), lambda i, j: (i, 0)),
                  _resident((bm, SSM_WIDTH), lambda i, j: (i, 0)),
                  pl.BlockSpec((bm, bn), lambda i, j: (i, gcol + j)),
                  pl.BlockSpec((bm, bn), lambda i, j: (i, gcol + gstep + j)),
                  pl.BlockSpec((bm, bn), lambda i, j: (i, gcol + 2 * gstep + j)),
                  wrows(ATTN_WIDTH, 0),
                  wrows(CONV_WIDTH, ATTN_WIDTH // CONV_WIDTH),
                  wrows(SSM_WIDTH, (ATTN_WIDTH + CONV_WIDTH) // SSM_WIDTH)],
        out_specs=pl.BlockSpec((bm, bn), lambda i, j: (i, j)),
        out_shape=jax.ShapeDtypeStruct((m, D_MODEL), BF16),
        compiler_params=_params(blk, tmp, ("arbitrary", "arbitrary"), _nbytes((bm, MIX_WIDTH), BF16)),
        name="branch_merge",
    )(o_attn, o_conv, o_ssm, proj, proj, proj, w_branch, w_branch, w_branch)


FFN_OUT_ROWS = 1024


def _ffn(x, normed, w_in, w_out, layer, next_gain):
    act = swiglu_in(*normed, w_in, layer)
    return matmul_res(act, w_out, layer, x, 0.5, bm=FFN_OUT_ROWS, next_gain=next_gain)


def _layer(x, normed, seq, bias, ssm_ops, p, layer):
    at = lambda name: p[name][layer]
    x, *normed = _ffn(x, normed, p['w_ffn1_in'], p['w_ffn1_out'], layer, at('mix_norm'))
    proj = matmul(*normed, p['w_in'], layer, F32)
    o_attn = attention(proj, bias, at('q_norm'), at('k_norm'), at('attn_sinks'), seq)
    o_conv = conv_branch(proj, at('conv_dw'), at('conv_dw_bias'), at('conv_ln_g'), at('conv_ln_b'),
                         at('conv_pw').astype(BF16), seq)
    y_ssm = ssm_branch(proj, ssm_ops, p['ssm_d'], layer, seq)
    o_ssm = glu_matmul(y_ssm, p['ssm_glu'], layer)
    merged = branch_merge(o_attn, o_conv, o_ssm, proj, p['w_branch'], layer)
    x, *normed = matmul_res(merged, p['w_out'], layer, x, 1.0, bm=2048, next_gain=at('ffn2_norm'))
    if layer + 1 == DEPTH:
        return _ffn(x, normed, p['w_ffn2_in'], p['w_ffn2_out'], layer, None), None
    x, *normed = _ffn(x, normed, p['w_ffn2_in'], p['w_ffn2_out'], layer, p['ffn1_norm'][layer + 1])
    return x, normed


def kernel(x, rel_bias, ffn1_norm, w_ffn1_in, w_ffn1_out, mix_norm, w_in, q_norm, k_norm, attn_sinks,
           conv_dw, conv_dw_bias, conv_ln_g, conv_ln_b, conv_pw, ssm_a_re, ssm_a_im, ssm_log_dt,
           ssm_b_re, ssm_b_im, ssm_c_re, ssm_c_im, ssm_d, ssm_glu, w_branch, w_out, ffn2_norm,
           w_ffn2_in, w_ffn2_out):
    layer_params = dict(
        ffn1_norm=ffn1_norm, w_ffn1_in=w_ffn1_in, w_ffn1_out=w_ffn1_out, mix_norm=mix_norm, w_in=w_in,
        q_norm=q_norm, k_norm=k_norm, attn_sinks=attn_sinks, conv_dw=conv_dw, conv_dw_bias=conv_dw_bias,
        conv_ln_g=conv_ln_g, conv_ln_b=conv_ln_b, conv_pw=conv_pw, ssm_a_re=ssm_a_re, ssm_a_im=ssm_a_im,
        ssm_log_dt=ssm_log_dt, ssm_b_re=ssm_b_re, ssm_b_im=ssm_b_im, ssm_c_re=ssm_c_re, ssm_c_im=ssm_c_im,
        ssm_d=ssm_d, ssm_glu=ssm_glu, w_branch=w_branch, w_out=w_out, ffn2_norm=ffn2_norm,
        w_ffn2_in=w_ffn2_in, w_ffn2_out=w_ffn2_out)
    b, seq, d = x.shape
    bias = band_bias(rel_bias)
    ssm_ops = jax.vmap(functools.partial(ssm_operands, n_chunks=seq // SSM_CHUNK))(
        ssm_a_re, ssm_a_im, ssm_log_dt, ssm_b_re, ssm_b_im, ssm_c_re, ssm_c_im)
    xf = x.reshape(b * seq, d)
    normed = norm_prep(xf, ffn1_norm[0])
    for layer in range(DEPTH):
        xf, normed = _layer(xf, normed, seq, bias, ssm_ops, layer_params, layer)
    return xf.reshape(b, seq, d)
```

```python
import functools
import math

import jax
import jax.numpy as jnp
from jax import lax
from jax.experimental import pallas as pl
from jax.experimental.pallas import tpu as pltpu

F32 = jnp.float32
BF16 = jnp.bfloat16

D_MODEL = 4096
DEPTH = 2
N_Q_HEADS = 32
N_KV_HEADS = 8
HEAD_DIM = 64
Q_GROUP = N_Q_HEADS // N_KV_HEADS
ATTN_WIDTH = N_Q_HEADS * HEAD_DIM
KV_WIDTH = N_KV_HEADS * HEAD_DIM
WINDOW = 128
BLOCK = 128
N_BUCKETS = 32
MAX_DISTANCE = 128
CONV_WIDTH = D_MODEL // 4
CONV_KERNEL = 31
SSM_WIDTH = D_MODEL // 4
SSM_GROUP = 16
SSM_GROUPS = SSM_WIDTH // SSM_GROUP
SSM_STATE = 64
D_FF = 256 * ((8 * D_MODEL // 3 + 255) // 256)
N_BRANCH = 3
MIX_WIDTH = ATTN_WIDTH + CONV_WIDTH + SSM_WIDTH
IN_WIDTH = ATTN_WIDTH + 2 * KV_WIDTH + 2 * CONV_WIDTH + SSM_WIDTH + N_BRANCH * D_MODEL
EPS = 1e-6

OFF_K = ATTN_WIDTH
OFF_V = OFF_K + KV_WIDTH
OFF_CONV = OFF_V + KV_WIDTH
OFF_SSM = OFF_CONV + 2 * CONV_WIDTH
OFF_GATE = OFF_SSM + SSM_WIDTH

V7X_LANES = 128
V7X_SUBLANES = 8
V7X_VMEM_BYTES = 64 * 1024 * 1024
V7X_VMEM_INTERNAL_BYTES = 12 * 1024 * 1024

SSM_CHUNK = 8
SSM_QGROUPS = 8
SSM_QCH = SSM_QGROUPS * SSM_GROUP
SSM_NQ = SSM_GROUPS // SSM_QGROUPS
SSM_QSTATE = SSM_QGROUPS * SSM_STATE


def _nbytes(shape, dtype):
    return math.prod(shape) * jnp.dtype(dtype).itemsize


def _params(block_bytes, scratch_bytes=0, semantics=None, single_bytes=0):
    limit = 2 * block_bytes + single_bytes + scratch_bytes + V7X_VMEM_INTERNAL_BYTES
    limit = min(limit, V7X_VMEM_BYTES - 4 * 1024 * 1024)
    return pltpu.CompilerParams(dimension_semantics=semantics, vmem_limit_bytes=int(limit))


def _resident(shape, index_map):
    return pl.BlockSpec(shape, index_map, pipeline_mode=pl.Buffered(1))


def _weight(k, bn, layer, col):
    return pl.BlockSpec((None, k, bn), lambda i, j: (layer, 0, col(j)))


def _wdot(a, w_ref):
    return jnp.dot(a, w_ref[...].astype(BF16), preferred_element_type=F32)


ROW_SUB = 256


def _row_blocks(bm):
    sub = min(ROW_SUB, bm)
    return [slice(r, r + sub) for r in range(0, bm, sub)]


def _lane_partial_sq(y):
    sq = y * y
    part = sq[:, :V7X_LANES]
    for c in range(1, y.shape[1] // V7X_LANES):
        part = part + sq[:, c * V7X_LANES:(c + 1) * V7X_LANES]
    return part


def _row_scale(ssq_ref, d):
    return lax.rsqrt(jnp.sum(ssq_ref[...], axis=-1, keepdims=True) * (1.0 / d) + EPS)


def _norm_prep_kernel(x_ref, g_ref, xg_ref, ssq_ref):
    x = x_ref[...]
    xg_ref[...] = (x * g_ref[...]).astype(xg_ref.dtype)
    ssq_ref[...] = _lane_partial_sq(x)


def norm_prep(x, g, bm=256):
    m, d = x.shape
    blk = (_nbytes((bm, d), F32) + _nbytes((bm, d), BF16) + _nbytes((1, d), F32)
           + _nbytes((bm, V7X_LANES), F32))
    return pl.pallas_call(
        _norm_prep_kernel,
        grid=(m // bm,),
        in_specs=[pl.BlockSpec((bm, d), lambda i: (i, 0)),
                  pl.BlockSpec((1, d), lambda i: (0, 0))],
        out_specs=[pl.BlockSpec((bm, d), lambda i: (i, 0)),
                   pl.BlockSpec((bm, V7X_LANES), lambda i: (i, 0))],
        out_shape=[jax.ShapeDtypeStruct((m, d), BF16),
                   jax.ShapeDtypeStruct((m, V7X_LANES), F32)],
        compiler_params=_params(blk, semantics=("arbitrary",)),
        name="norm_prep",
    )(x, g.reshape(1, d))


def _swiglu_kernel(xg_ref, ssq_ref, wg_ref, wu_ref, o_ref, r_scr):
    @pl.when(pl.program_id(1) == 0)
    def _():
        r_scr[...] = _row_scale(ssq_ref, xg_ref.shape[1])

    wg = wg_ref[...].astype(BF16)
    wu = wu_ref[...].astype(BF16)
    for rows in _row_blocks(xg_ref.shape[0]):
        h = xg_ref[rows, :]
        r = r_scr[rows, :]
        g = r * jnp.dot(h, wg, preferred_element_type=F32)
        u = r * jnp.dot(h, wu, preferred_element_type=F32)
        o_ref[rows, :] = (g * jax.nn.sigmoid(g) * u).astype(o_ref.dtype)


def swiglu_in(xg, ssq, w, layer, bm=2048, bn=256):
    m, k = xg.shape
    f = w.shape[2] // 2
    bm = min(bm, m)
    nj = f // bn
    blk = 2 * _nbytes((k, bn), F32) + _nbytes((bm, bn), BF16)
    tmp = 2 * _nbytes((k, bn), BF16) + 2 * _nbytes((bm, bn), F32) + _nbytes((bm, V7X_LANES), F32)
    single = _nbytes((bm, k), BF16) + _nbytes((bm, V7X_LANES), F32)
    return pl.pallas_call(
        _swiglu_kernel,
        grid=(m // bm, nj),
        in_specs=[_resident((bm, k), lambda i, j: (i, 0)),
                  _resident((bm, V7X_LANES), lambda i, j: (i, 0)),
                  _weight(k, bn, layer, lambda j: j),
                  _weight(k, bn, layer, lambda j: j + nj)],
        out_specs=pl.BlockSpec((bm, bn), lambda i, j: (i, j)),
        out_shape=jax.ShapeDtypeStruct((m, f), BF16),
        scratch_shapes=[pltpu.VMEM((bm, 1), F32)],
        compiler_params=_params(blk, tmp, ("arbitrary", "arbitrary"), single),
        name="swiglu_in",
    )(xg, ssq, w, w)


def _matmul_kernel(xg_ref, ssq_ref, w_ref, o_ref, r_scr):
    @pl.when(pl.program_id(1) == 0)
    def _():
        r_scr[...] = _row_scale(ssq_ref, xg_ref.shape[1])

    w = w_ref[...].astype(BF16)
    for rows in _row_blocks(xg_ref.shape[0]):
        y = r_scr[rows, :] * jnp.dot(xg_ref[rows, :], w, preferred_element_type=F32)
        o_ref[rows, :] = y.astype(o_ref.dtype)


def matmul(xg, ssq, w, layer, out_dtype, bm=2048, bn=512):
    m, k = xg.shape
    n = w.shape[2]
    bm, bn = min(bm, m), min(bn, n)
    blk = _nbytes((k, bn), F32) + _nbytes((bm, bn), out_dtype)
    tmp = _nbytes((k, bn), BF16) + _nbytes((bm, bn), F32) + _nbytes((bm, V7X_LANES), F32)
    single = _nbytes((bm, k), BF16) + _nbytes((bm, V7X_LANES), F32)
    return pl.pallas_call(
        _matmul_kernel,
        grid=(m // bm, n // bn),
        in_specs=[_resident((bm, k), lambda i, j: (i, 0)),
                  _resident((bm, V7X_LANES), lambda i, j: (i, 0)),
                  _weight(k, bn, layer, lambda j: j)],
        out_specs=pl.BlockSpec((bm, bn), lambda i, j: (i, j)),
        out_shape=jax.ShapeDtypeStruct((m, n), out_dtype),
        scratch_shapes=[pltpu.VMEM((bm, 1), F32)],
        compiler_params=_params(blk, tmp, ("arbitrary", "arbitrary"), single),
        name="matmul",
    )(xg, ssq, w)


def _matmul_res_kernel(a_ref, w_ref, x_ref, o_ref, *, scale):
    w = w_ref[...].astype(BF16)
    for rows in _row_blocks(a_ref.shape[0]):
        o_ref[rows, :] = x_ref[rows, :] + scale * jnp.dot(a_ref[rows, :], w, preferred_element_type=F32)


def _matmul_res_norm_kernel(a_ref, w_ref, x_ref, g_ref, o_ref, xg_ref, ssq_ref, *, scale):
    @pl.when(pl.program_id(1) == 0)
    def _():
        ssq_ref[...] = jnp.zeros(ssq_ref.shape, F32)

    w = w_ref[...].astype(BF16)
    for rows in _row_blocks(a_ref.shape[0]):
        y = x_ref[rows, :] + scale * jnp.dot(a_ref[rows, :], w, preferred_element_type=F32)
        o_ref[rows, :] = y
        xg_ref[rows, :] = (y * g_ref[...]).astype(xg_ref.dtype)
        ssq_ref[rows, :] += _lane_partial_sq(y)


def matmul_res(a, w, layer, x, scale, bm, next_gain=None, bn=256):
    m, k = a.shape
    n = w.shape[2]
    bm, bn = min(bm, m), min(bn, n)
    tile = pl.BlockSpec((bm, bn), lambda i, j: (i, j))
    in_specs = [_resident((bm, k), lambda i, j: (i, 0)), _weight(k, bn, layer, lambda j: j), tile]
    blk = _nbytes((k, bn), F32) + 2 * _nbytes((bm, bn), F32)
    tmp = _nbytes((k, bn), BF16) + _nbytes((bm, bn), F32)
    single = _nbytes((bm, k), a.dtype)
    sem = ("arbitrary", "arbitrary")
    if next_gain is None:
        return pl.pallas_call(
            functools.partial(_matmul_res_kernel, scale=scale),
            grid=(m // bm, n // bn),
            in_specs=in_specs,
            out_specs=tile,
            out_shape=jax.ShapeDtypeStruct((m, n), F32),
            compiler_params=_params(blk, tmp, sem, single),
            name="matmul_res",
        )(a, w, x)
    blk += _nbytes((1, bn), F32) + _nbytes((bm, bn), BF16) + _nbytes((bm, V7X_LANES), F32)
    return pl.pallas_call(
        functools.partial(_matmul_res_norm_kernel, scale=scale),
        grid=(m // bm, n // bn),
        in_specs=in_specs + [pl.BlockSpec((1, bn), lambda i, j: (0, j))],
        out_specs=[tile, tile, pl.BlockSpec((bm, V7X_LANES), lambda i, j: (i, 0))],
        out_shape=[jax.ShapeDtypeStruct((m, n), F32), jax.ShapeDtypeStruct((m, n), BF16),
                   jax.ShapeDtypeStruct((m, V7X_LANES), F32)],
        compiler_params=_params(blk, tmp, sem, single),
        name="matmul_res_norm",
    )(a, w, x, next_gain.astype(F32).reshape(1, n))


def _glu_kernel(a_ref, wa_ref, wb_ref, o_ref):
    wa = wa_ref[...].astype(BF16)
    wb = wb_ref[...].astype(BF16)
    for rows in _row_blocks(a_ref.shape[0]):
        a = a_ref[rows, :].astype(BF16)
        za = jnp.dot(a, wa, preferred_element_type=F32)
        zb = jnp.dot(a, wb, preferred_element_type=F32)
        o_ref[rows, :] = (za * jax.nn.sigmoid(zb)).astype(o_ref.dtype)


def glu_matmul(a, w, layer, bm=2048, bn=256):
    m, k = a.shape
    n = w.shape[2] // 2
    bm = min(bm, m)
    nj = n // bn
    blk = 2 * _nbytes((k, bn), F32) + _nbytes((bm, bn), BF16)
    tmp = _nbytes((bm, k), BF16) + 2 * _nbytes((k, bn), BF16) + 2 * _nbytes((bm, bn), F32)
    return pl.pallas_call(
        _glu_kernel,
        grid=(m // bm, nj),
        in_specs=[_resident((bm, k), lambda i, j: (i, 0)),
                  _weight(k, bn, layer, lambda j: j),
                  _weight(k, bn, layer, lambda j: j + nj)],
        out_specs=pl.BlockSpec((bm, bn), lambda i, j: (i, j)),
        out_shape=jax.ShapeDtypeStruct((m, n), BF16),
        compiler_params=_params(blk, tmp, ("arbitrary", "arbitrary"), _nbytes((bm, k), a.dtype)),
        name="glu_matmul",
    )(a, w, w)


LOG2E = math.log2(math.e)


def _segment_mean_sq(x, seg_ref):
    sq = x * x
    hi = sq.astype(BF16)
    lo = (sq - hi.astype(F32)).astype(BF16)
    s = (jnp.dot(hi, seg_ref[...], preferred_element_type=F32)
         + jnp.dot(lo, seg_ref[...], preferred_element_type=F32))
    return s * (1.0 / HEAD_DIM)


def _attn_kernel(sink_ref, q_ref, kp_ref, kc_ref, vp_ref, vc_ref, bias_ref, qg_ref, kg_ref,
                 seg_ref, o_ref, s_scr, p_scr, sink_scr):
    lanes = V7X_LANES
    low = lax.broadcasted_iota(jnp.int32, (1, lanes), 1) < HEAD_DIM
    high = jnp.logical_not(low)

    kk = jnp.concatenate([kp_ref[...], kc_ref[...]], axis=0)
    vv = jnp.concatenate([vp_ref[...], vc_ref[...]], axis=0)
    ones = jnp.ones((HEAD_DIM, 2 * BLOCK), F32)

    k_low, k_high, v_low, v_high = [], [], [], []
    for c in range(KV_WIDTH // lanes):
        kc = kk[:, c * lanes:(c + 1) * lanes]
        kc = kc * lax.rsqrt(_segment_mean_sq(kc, seg_ref) + EPS) * kg_ref[:, c * lanes:(c + 1) * lanes]
        ks = pltpu.roll(kc, HEAD_DIM, axis=1)
        vt = vv[:, c * lanes:(c + 1) * lanes].T
        for odd in range(2):
            k_low.append(jnp.where(low, ks if odd else kc, 0.0).astype(BF16))
            k_high.append(jnp.where(high, kc if odd else ks, 0.0).astype(BF16))
            vth = vt[odd * HEAD_DIM:(odd + 1) * HEAD_DIM, :]
            v_low.append(jnp.concatenate([vth, ones], axis=0).astype(BF16))
            v_high.append(jnp.concatenate([ones, vth], axis=0).astype(BF16))

    n_pairs = ATTN_WIDTH // lanes
    for pair in range(n_pairs):
        kv = pair // (Q_GROUP // 2)
        qt = q_ref[:, pair * lanes:(pair + 1) * lanes].T
        ms = [jnp.mean(jnp.square(qt[h * HEAD_DIM:(h + 1) * HEAD_DIM]), axis=0, keepdims=True)
              for h in range(2)]
        rs = jnp.concatenate([jnp.broadcast_to(lax.rsqrt(m + EPS), (HEAD_DIM, BLOCK)) for m in ms], axis=0)
        qt = (qt * rs * qg_ref[...]).astype(BF16)
        for half, kh in enumerate((k_low[kv], k_high[kv])):
            head = 2 * pair + half
            s_scr[head] = jnp.dot(kh, qt, preferred_element_type=F32) + bias_ref[head]

    for head in range(N_Q_HEADS):
        s = s_scr[head]
        sink = sink_ref[head]
        mx = jnp.maximum(jnp.max(s, axis=0, keepdims=True), sink)
        p_scr[head] = jnp.exp2(s - mx).astype(BF16)
        sink_scr[head:head + 1, :] = jnp.exp2(sink - mx)

    for pair in range(n_pairs):
        kv = pair // (Q_GROUP // 2)
        num, den = [], []
        for half, vh in enumerate((v_low[kv], v_high[kv])):
            head = 2 * pair + half
            o = jnp.dot(vh, p_scr[head], preferred_element_type=F32)
            lo_rows, hi_rows = o[:HEAD_DIM], o[HEAD_DIM:]
            num.append(lo_rows if half == 0 else hi_rows)
            den.append((hi_rows if half == 0 else lo_rows) + sink_scr[head:head + 1, :])
        ot = jnp.concatenate(num, axis=0) / jnp.concatenate(den, axis=0)
        o_ref[:, pair * lanes:(pair + 1) * lanes] = ot.T.astype(o_ref.dtype)


def attention(proj, bias, q_gain, k_gain, sinks, seq):
    m = proj.shape[0]
    nb = seq // BLOCK
    nblk = m // BLOCK
    lanes = V7X_LANES
    seg_id = jnp.arange(lanes) // HEAD_DIM
    seg = (seg_id[:, None] == seg_id[None, :]).astype(BF16)
    qg = jnp.broadcast_to(jnp.tile(q_gain.astype(F32) * (HEAD_DIM ** -0.5 * LOG2E), 2)[:, None], (lanes, BLOCK))
    kg = jnp.tile(k_gain.astype(F32), N_KV_HEADS).reshape(1, KV_WIDTH)
    kcol, vcol = OFF_K // KV_WIDTH, OFF_V // KV_WIDTH
    table = (None,) + bias.shape[1:]

    def prev(i):
        return jnp.where(i % nb == 0, i, i - 1)

    blk = (_nbytes((BLOCK, ATTN_WIDTH), F32) + 4 * _nbytes((BLOCK, KV_WIDTH), F32)
           + _nbytes(bias.shape[1:], F32) + _nbytes((BLOCK, ATTN_WIDTH), BF16))
    scratch = (_nbytes((N_Q_HEADS, 2 * BLOCK, BLOCK), F32) + _nbytes((N_Q_HEADS, 2 * BLOCK, BLOCK), BF16)
               + _nbytes((N_Q_HEADS, BLOCK), F32))
    return pl.pallas_call(
        _attn_kernel,
        grid=(nblk,),
        in_specs=[pl.BlockSpec(memory_space=pltpu.SMEM),
                  pl.BlockSpec((BLOCK, ATTN_WIDTH), lambda i: (i, 0)),
                  pl.BlockSpec((BLOCK, KV_WIDTH), lambda i: (prev(i), kcol)),
                  pl.BlockSpec((BLOCK, KV_WIDTH), lambda i: (i, kcol)),
                  pl.BlockSpec((BLOCK, KV_WIDTH), lambda i: (prev(i), vcol)),
                  pl.BlockSpec((BLOCK, KV_WIDTH), lambda i: (i, vcol)),
                  pl.BlockSpec(table, lambda i: (jnp.where(i % nb == 0, 0, 1), 0, 0, 0)),
                  pl.BlockSpec((lanes, BLOCK), lambda i: (0, 0)),
                  pl.BlockSpec((1, KV_WIDTH), lambda i: (0, 0)),
                  pl.BlockSpec((lanes, lanes), lambda i: (0, 0))],
        out_specs=pl.BlockSpec((BLOCK, ATTN_WIDTH), lambda i: (i, 0)),
        out_shape=jax.ShapeDtypeStruct((m, ATTN_WIDTH), BF16),
        scratch_shapes=[pltpu.VMEM((N_Q_HEADS, 2 * BLOCK, BLOCK), F32),
                        pltpu.VMEM((N_Q_HEADS, 2 * BLOCK, BLOCK), BF16),
                        pltpu.VMEM((N_Q_HEADS, BLOCK), F32)],
        compiler_params=_params(blk, scratch, semantics=("arbitrary",)),
        name="swa_attention",
    )(sinks.astype(F32) * LOG2E, proj, proj, proj, proj, proj, bias, qg, kg, seg)


def _t5_bucket(dist):
    max_exact = N_BUCKETS // 2
    n = jnp.maximum(dist, 0)
    ratio = jnp.log(jnp.maximum(n, 1).astype(F32) / max_exact) / math.log(MAX_DISTANCE / max_exact)
    large = jnp.minimum(max_exact + (ratio * (N_BUCKETS - max_exact)).astype(jnp.int32), N_BUCKETS - 1)
    return jnp.where(n < max_exact, n, large)


def band_bias(rel_bias):
    keys, period = 2 * BLOCK, 4 * BLOCK
    by_dist = rel_bias.astype(F32)[_t5_bucket(jnp.arange(WINDOW))].T * LOG2E
    v = jnp.concatenate([by_dist, jnp.full((N_Q_HEADS, period - WINDOW), -jnp.inf, F32)], axis=1)
    a = jnp.tile(v, (1, keys))[:, :keys * (period - 1)].reshape(N_Q_HEADS, keys, period - 1)
    rest = a[:, :, BLOCK:2 * BLOCK]
    c = jnp.arange(keys)[None, :, None]
    first = jnp.where(c >= BLOCK, rest, -jnp.inf)
    return jnp.stack([first, rest])


CONV_HALO = 32
CONV_ROWS = 32


def _conv_kernel(a_ref, g_ref, w_ref, b_ref, lg_ref, lb_ref, pw_ref, o_ref, ubuf, shbuf, ybuf, *, nt, tt):
    t = pl.program_id(0) % nt

    @pl.when(t == 0)
    def _():
        ubuf[0:CONV_HALO, :] = jnp.zeros((CONV_HALO, CONV_WIDTH), F32)

    @pl.when(t != 0)
    def _():
        ubuf[0:CONV_HALO, :] = ubuf[tt:tt + CONV_HALO, :]

    ubuf[CONV_HALO:CONV_HALO + tt, :] = a_ref[...] * jax.nn.sigmoid(g_ref[...])

    first = CONV_HALO - (CONV_KERNEL - 1)
    cw = 2 * V7X_LANES

    span = tt + CONV_HALO - V7X_SUBLANES
    for r in range(1, V7X_SUBLANES):
        shbuf[r - 1, 0:span, :] = ubuf[r:r + span, :]

    def rows(rb, carry):
        base = pl.multiple_of(rb * CONV_ROWS, CONV_ROWS)
        for cb in range(CONV_WIDTH // cw):
            cols = slice(cb * cw, (cb + 1) * cw)
            acc = jnp.zeros((CONV_ROWS, cw), F32) + b_ref[:, cols]
            for j in range(CONV_KERNEL):
                r, a = (first + j) % V7X_SUBLANES, (first + j) // V7X_SUBLANES * V7X_SUBLANES
                tap = (ubuf[pl.ds(base + a, CONV_ROWS), cols] if r == 0
                       else shbuf[r - 1, pl.ds(base + a, CONV_ROWS), cols])
                acc = acc + w_ref[j:j + 1, cols] * tap
            ybuf[pl.ds(base, CONV_ROWS), cols] = acc
        return carry

    lax.fori_loop(0, tt // CONV_ROWS, rows, 0)

    y = ybuf[...]
    mu = jnp.mean(y, axis=-1, keepdims=True)
    yc = y - mu
    var = jnp.mean(yc * yc, axis=-1, keepdims=True)
    z = yc * lax.rsqrt(var + EPS) * lg_ref[...] + lb_ref[...]
    z = z * jax.nn.sigmoid(z)
    o_ref[...] = jnp.dot(z.astype(BF16), pw_ref[...], preferred_element_type=F32).astype(o_ref.dtype)


def conv_branch(proj, w_dw, b_dw, ln_g, ln_b, w_pw, seq, tt=512):
    m = proj.shape[0]
    tt = min(tt, seq)
    nt = seq // tt
    acol = OFF_CONV // CONV_WIDTH
    w_pad = jnp.zeros((CONV_HALO, CONV_WIDTH), F32).at[:CONV_KERNEL].set(w_dw.astype(F32))
    row = lambda v: v.astype(F32).reshape(1, CONV_WIDTH)
    blk = (2 * _nbytes((tt, CONV_WIDTH), F32) + _nbytes((CONV_HALO, CONV_WIDTH), F32)
           + 3 * _nbytes((1, CONV_WIDTH), F32) + _nbytes((CONV_WIDTH, CONV_WIDTH), BF16)
           + _nbytes((tt, CONV_WIDTH), BF16))
    scratch = V7X_SUBLANES * _nbytes((tt + CONV_HALO, CONV_WIDTH), F32) + _nbytes((tt, CONV_WIDTH), F32)
    return pl.pallas_call(
        functools.partial(_conv_kernel, nt=nt, tt=tt),
        grid=(m // tt,),
        in_specs=[pl.BlockSpec((tt, CONV_WIDTH), lambda i: (i, acol)),
                  pl.BlockSpec((tt, CONV_WIDTH), lambda i: (i, acol + 1)),
                  pl.BlockSpec((CONV_HALO, CONV_WIDTH), lambda i: (0, 0)),
                  pl.BlockSpec((1, CONV_WIDTH), lambda i: (0, 0)),
                  pl.BlockSpec((1, CONV_WIDTH), lambda i: (0, 0)),
                  pl.BlockSpec((1, CONV_WIDTH), lambda i: (0, 0)),
                  pl.BlockSpec((CONV_WIDTH, CONV_WIDTH), lambda i: (0, 0))],
        out_specs=pl.BlockSpec((tt, CONV_WIDTH), lambda i: (i, 0)),
        out_shape=jax.ShapeDtypeStruct((m, CONV_WIDTH), BF16),
        scratch_shapes=[pltpu.VMEM((tt + CONV_HALO, CONV_WIDTH), F32),
                        pltpu.VMEM((V7X_SUBLANES - 1, tt + CONV_HALO, CONV_WIDTH), F32),
                        pltpu.VMEM((tt, CONV_WIDTH), F32)],
        compiler_params=_params(blk, scratch, semantics=("arbitrary",)),
        name="conv_branch",
    )(proj, proj, w_pad, row(b_dw), row(ln_g), row(ln_b), w_pw)


def _swap_re_im(h):
    lanes = V7X_LANES
    return jnp.concatenate([pltpu.roll(h[:, k * lanes:(k + 1) * lanes], SSM_STATE, axis=1)
                            for k in range(h.shape[1] // lanes)], axis=1)


def _expand_operands(e_ref, f_ref, ws_scr, wi_scr, wxt_scr):
    T, rows_q, lanes = SSM_CHUNK, SSM_QCH, V7X_LANES
    wide = SSM_QGROUPS * lanes
    r = lax.broadcasted_iota(jnp.int32, (rows_q, wide), 0)
    c = lax.broadcasted_iota(jnp.int32, (rows_q, wide), 1)
    own_slab = (r // SSM_GROUP) == (c // lanes)
    r2 = lax.broadcasted_iota(jnp.int32, (rows_q, rows_q), 0)
    c2 = lax.broadcasted_iota(jnp.int32, (rows_q, rows_q), 1)
    same_group = (r2 // SSM_GROUP) == (c2 // SSM_GROUP)

    def group_diag(blk):
        return jnp.where(own_slab, jnp.concatenate([blk] * SSM_QGROUPS, axis=1), 0.0).astype(BF16)

    for s in range(T):
        ws_scr[s * rows_q:(s + 1) * rows_q, :] = group_diag(e_ref[0, T - 1 - s])
        wxt_scr[s * rows_q:(s + 1) * rows_q, :] = group_diag(f_ref[0, s + 1])
    f0 = f_ref[0, 0]
    zero = jnp.zeros((rows_q, rows_q), BF16)
    kbd = []
    for l in range(T):
        k_l = lax.dot_general(e_ref[0, l], f0, (((1,), (1,)), ((), ())), preferred_element_type=F32,
                              precision=lax.Precision.HIGHEST)
        kbd.append(jnp.where(same_group, k_l, 0.0).astype(BF16))
    for s in range(T):
        for t in range(T):
            wi_scr[s * rows_q:(s + 1) * rows_q, t * rows_q:(t + 1) * rows_q] = kbd[t - s] if t >= s else zero


def _ssm_kernel(u_ref, e_ref, f_ref, a1_ref, a2_ref, d_ref, o_ref, ws_scr, wi_scr, wxt_scr, *, nc):
    @pl.when(pl.program_id(1) == 0)
    def _():
        _expand_operands(e_ref, f_ref, ws_scr, wi_scr, wxt_scr)

    x = jnp.concatenate([u_ref[pl.ds(t, nc, stride=SSM_CHUNK), :] for t in range(SSM_CHUNK)], axis=1)
    xb = x.astype(BF16)
    h = jnp.dot(xb, ws_scr[...], preferred_element_type=F32)
    row = lax.broadcasted_iota(jnp.int32, (nc, 1), 0)
    level = 0
    while (1 << level) < nc:
        d = 1 << level
        prev = jnp.where(row >= d, pltpu.roll(h, d, axis=0), 0.0)
        h = h + a1_ref[0, level:level + 1, :] * prev + a2_ref[0, level:level + 1, :] * _swap_re_im(prev)
        level += 1
    hp = jnp.where(row >= 1, pltpu.roll(h, 1, axis=0), 0.0).astype(BF16)
    y = (jnp.dot(xb, wi_scr[...], preferred_element_type=F32)
         + lax.dot_general(hp, wxt_scr[...], (((1,), (1,)), ((), ())), preferred_element_type=F32)
         + d_ref[0] * x)
    y = jax.nn.gelu(y)
    for t in range(SSM_CHUNK):
        o_ref[pl.ds(t, nc, stride=SSM_CHUNK), :] = y[:, t * SSM_QCH:(t + 1) * SSM_QCH]


def ssm_branch(proj, operands, d_skip, layer, seq):
    e_blk, f_blk, a1, a2 = operands
    m = proj.shape[0]
    nseq = m // seq
    nc = seq // SSM_CHUNK
    ucol = OFF_SSM // SSM_QCH
    kw = SSM_CHUNK * SSM_QCH
    sw = 2 * SSM_QSTATE
    nlev = a1.shape[2]
    d_t = jnp.tile(d_skip.astype(F32).reshape(-1, SSM_NQ, 1, SSM_QCH), (1, 1, 1, SSM_CHUNK))
    blk = (2 * _nbytes((seq, SSM_QCH), F32) + _nbytes(e_blk.shape[2:], F32) + _nbytes(f_blk.shape[2:], F32)
           + 2 * _nbytes((nlev, sw), F32) + _nbytes((1, kw), F32))
    scratch = 2 * _nbytes((kw, sw), BF16) + _nbytes((kw, kw), BF16)
    tmp = 6 * _nbytes((nc, kw), F32)

    def per_q(shape):
        return pl.BlockSpec((None, 1) + shape, lambda q, b: (layer, q) + (0,) * len(shape))

    return pl.pallas_call(
        functools.partial(_ssm_kernel, nc=nc),
        grid=(SSM_NQ, nseq),
        in_specs=[pl.BlockSpec((seq, SSM_QCH), lambda q, b: (b, ucol + q)),
                  per_q(e_blk.shape[2:]), per_q(f_blk.shape[2:]),
                  per_q((nlev, sw)), per_q((nlev, sw)), per_q((1, kw))],
        out_specs=pl.BlockSpec((seq, SSM_QCH), lambda q, b: (b, q)),
        out_shape=jax.ShapeDtypeStruct((m, SSM_WIDTH), F32),
        scratch_shapes=[pltpu.VMEM((kw, sw), BF16), pltpu.VMEM((kw, kw), BF16), pltpu.VMEM((kw, sw), BF16)],
        compiler_params=_params(blk, scratch + tmp, semantics=("arbitrary", "arbitrary")),
        name="ssm_scan",
    )(proj, e_blk, f_blk, a1, a2, d_t)


def _cmul(ar, ai, br, bi):
    return ar * br - ai * bi, ar * bi + ai * br


def ssm_operands(a_re, a_im, log_dt, b_re, b_im, c_re, c_im, n_chunks):
    f32 = F32
    dt = jnp.exp(log_dt.astype(f32))[:, None]
    lam_re, lam_im = a_re.astype(f32), a_im.astype(f32)
    mag = jnp.exp(dt * lam_re)
    ang = dt * lam_im
    lb_re, lb_im = mag * jnp.cos(ang), mag * jnp.sin(ang)
    nr = lb_re - 1.0
    den = lam_re * lam_re + lam_im * lam_im
    coef_re = (nr * lam_re + lb_im * lam_im) / den
    coef_im = (lb_im * lam_re - nr * lam_im) / den
    br, bi = b_re.astype(f32), b_im.astype(f32)
    bb_re = coef_re[..., None] * br - coef_im[..., None] * bi
    bb_im = coef_re[..., None] * bi + coef_im[..., None] * br
    cr, ci = c_re.astype(f32), c_im.astype(f32)

    pw = [(jnp.ones_like(lb_re), jnp.zeros_like(lb_im))]
    for _ in range(SSM_CHUNK):
        pw.append(_cmul(pw[-1][0], pw[-1][1], lb_re, lb_im))

    T, Q, C, P = SSM_CHUNK, SSM_QGROUPS, SSM_GROUP, SSM_STATE
    rows_q = Q * C
    pw_re = jnp.stack([p[0] for p in pw])
    pw_im = jnp.stack([p[1] for p in pw])

    def per_tile(blocks):
        k = blocks.shape[0]
        return blocks.reshape(k, SSM_NQ, rows_q, 2 * P).transpose(1, 0, 2, 3)

    e_re, e_im = _cmul(pw_re[:T, :, :, None], pw_im[:T, :, :, None], bb_re, bb_im)
    e_blk = per_tile(jnp.concatenate([jnp.swapaxes(e_re, 2, 3), jnp.swapaxes(e_im, 2, 3)], axis=-1))
    f_re, f_im = _cmul(cr, ci, pw_re[:, :, None, :], pw_im[:, :, None, :])
    f_blk = per_tile(jnp.concatenate([f_re, -f_im], axis=-1))

    a = pw[T]
    a1, a2 = [], []
    level = 0
    while (1 << level) < n_chunks:
        a1.append(jnp.concatenate([a[0], a[0]], axis=-1).reshape(SSM_NQ, Q * 2 * P))
        a2.append(jnp.concatenate([-a[1], a[1]], axis=-1).reshape(SSM_NQ, Q * 2 * P))
        a = _cmul(a[0], a[1], a[0], a[1])
        level += 1
    return e_blk, f_blk, jnp.stack(a1, axis=1), jnp.stack(a2, axis=1)


def _merge_kernel(oa_ref, oc_ref, os_ref, ga_ref, gc_ref, gs_ref, wa_ref, wc_ref, ws_ref, o_ref):
    wa, wc, ws = (w[...].astype(BF16) for w in (wa_ref, wc_ref, ws_ref))
    for rows in _row_blocks(o_ref.shape[0]):
        ya = jnp.dot(oa_ref[rows, :], wa, preferred_element_type=F32)
        yc = jnp.dot(oc_ref[rows, :], wc, preferred_element_type=F32)
        ys = jnp.dot(os_ref[rows, :], ws, preferred_element_type=F32)
        merged = (jax.nn.sigmoid(ga_ref[rows, :]) * ya + jax.nn.sigmoid(gc_ref[rows, :]) * yc
                  + jax.nn.sigmoid(gs_ref[rows, :]) * ys)
        o_ref[rows, :] = merged.astype(o_ref.dtype)


def branch_merge(o_attn, o_conv, o_ssm, proj, w_branch, layer, bm=2048, bn=256):
    m = proj.shape[0]
    bm = min(bm, m)
    gcol = OFF_GATE // bn
    gstep = D_MODEL // bn
    blk = 3 * _nbytes((bm, bn), F32) + _nbytes((MIX_WIDTH, bn), F32) + _nbytes((bm, bn), BF16)
    tmp = _nbytes((MIX_WIDTH, bn), BF16) + 4 * _nbytes((bm, bn), F32)

    def wrows(rows, row_tile):
        return pl.BlockSpec((None, rows, bn), lambda i, j: (layer, row_tile, j))

    return pl.pallas_call(
        _merge_kernel,
        grid=(m // bm, D_MODEL // bn),
        in_specs=[_resident((bm, ATTN_WIDTH), lambda i, j: (i, 0)),
                  _resident((bm, CONV_WIDTH), lambda i, j: (i, 0)),
                  _resident((bm, SSM_WIDTH), lambda i, j: (i, 0)),
                  pl.BlockSpec((bm, bn), lambda i, j: (i, gcol + j)),
                  pl.BlockSpec((bm, bn), lambda i, j: (i, gcol + gstep + j)),
                  pl.BlockSpec((bm, bn), lambda i, j: (i, gcol + 2 * gstep + j)),
                  wrows(ATTN_WIDTH, 0),
                  wrows(CONV_WIDTH, ATTN_WIDTH // CONV_WIDTH),
                  wrows(SSM_WIDTH, (ATTN_WIDTH + CONV_WIDTH) // SSM_WIDTH)],
        out_specs=pl.BlockSpec((bm, bn), lambda i, j: (i, j)),
        out_shape=jax.ShapeDtypeStruct((m, D_MODEL), BF16),
        compiler_params=_params(blk, tmp, ("arbitrary", "arbitrary"), _nbytes((bm, MIX_WIDTH), BF16)),
        name="branch_merge",
    )(o_attn, o_conv, o_ssm, proj, proj, proj, w_branch, w_branch, w_branch)


FFN_OUT_ROWS = 1024


def _ffn(x, normed, w_in, w_out, layer, next_gain):
    act = swiglu_in(*normed, w_in, layer)
    return matmul_res(act, w_out, layer, x, 0.5, bm=FFN_OUT_ROWS, next_gain=next_gain)


def _layer(x, normed, seq, bias, ssm_ops, p, layer):
    at = lambda name: p[name][layer]
    x, *normed = _ffn(x, normed, p['w_ffn1_in'], p['w_ffn1_out'], layer, at('mix_norm'))
    proj = matmul(*normed, p['w_in'], layer, F32)
    o_attn = attention(proj, bias, at('q_norm'), at('k_norm'), at('attn_sinks'), seq)
    o_conv = conv_branch(proj, at('conv_dw'), at('conv_dw_bias'), at('conv_ln_g'), at('conv_ln_b'),
                         at('conv_pw').astype(BF16), seq)
    y_ssm = ssm_branch(proj, ssm_ops, p['ssm_d'], layer, seq)
    o_ssm = glu_matmul(y_ssm, p['ssm_glu'], layer)
    merged = branch_merge(o_attn, o_conv, o_ssm, proj, p['w_branch'], layer)
    x, *normed = matmul_res(merged, p['w_out'], layer, x, 1.0, bm=2048, next_gain=at('ffn2_norm'))
    if layer + 1 == DEPTH:
        return _ffn(x, normed, p['w_ffn2_in'], p['w_ffn2_out'], layer, None), None
    x, *normed = _ffn(x, normed, p['w_ffn2_in'], p['w_ffn2_out'], layer, p['ffn1_norm'][layer + 1])
    return x, normed


def kernel(x, rel_bias, ffn1_norm, w_ffn1_in, w_ffn1_out, mix_norm, w_in, q_norm, k_norm, attn_sinks,
           conv_dw, conv_dw_bias, conv_ln_g, conv_ln_b, conv_pw, ssm_a_re, ssm_a_im, ssm_log_dt,
           ssm_b_re, ssm_b_im, ssm_c_re, ssm_c_im, ssm_d, ssm_glu, w_branch, w_out, ffn2_norm,
           w_ffn2_in, w_ffn2_out):
    layer_params = dict(
        ffn1_norm=ffn1_norm, w_ffn1_in=w_ffn1_in, w_ffn1_out=w_ffn1_out, mix_norm=mix_norm, w_in=w_in,
        q_norm=q_norm, k_norm=k_norm, attn_sinks=attn_sinks, conv_dw=conv_dw, conv_dw_bias=conv_dw_bias,
        conv_ln_g=conv_ln_g, conv_ln_b=conv_ln_b, conv_pw=conv_pw, ssm_a_re=ssm_a_re, ssm_a_im=ssm_a_im,
        ssm_log_dt=ssm_log_dt, ssm_b_re=ssm_b_re, ssm_b_im=ssm_b_im, ssm_c_re=ssm_c_re, ssm_c_im=ssm_c_im,
        ssm_d=ssm_d, ssm_glu=ssm_glu, w_branch=w_branch, w_out=w_out, ffn2_norm=ffn2_norm,
        w_ffn2_in=w_ffn2_in, w_ffn2_out=w_ffn2_out)
    b, seq, d = x.shape
    bias = band_bias(rel_bias)
    ssm_ops = jax.vmap(functools.partial(ssm_operands, n_chunks=seq // SSM_CHUNK))(
        ssm_a_re, ssm_a_im, ssm_log_dt, ssm_b_re, ssm_b_im, ssm_c_re, ssm_c_im)
    xf = x.reshape(b * seq, d)
    normed = norm_prep(xf, ffn1_norm[0])
    for layer in range(DEPTH):
        xf, normed = _layer(xf, normed, seq, bias, ssm_ops, layer_params, layer)
    return xf.reshape(b, seq, d)
```

```python
import functools
import math

import jax
import jax.numpy as jnp
from jax import lax
from jax.experimental import pallas as pl
from jax.experimental.pallas import tpu as pltpu

F32 = jnp.float32
BF16 = jnp.bfloat16

D_MODEL = 4096
DEPTH = 2
N_Q_HEADS = 32
N_KV_HEADS = 8
HEAD_DIM = 64
Q_GROUP = N_Q_HEADS // N_KV_HEADS
ATTN_WIDTH = N_Q_HEADS * HEAD_DIM
KV_WIDTH = N_KV_HEADS * HEAD_DIM
WINDOW = 128
BLOCK = 128
N_BUCKETS = 32
MAX_DISTANCE = 128
CONV_WIDTH = D_MODEL // 4
CONV_KERNEL = 31
SSM_WIDTH = D_MODEL // 4
SSM_GROUP = 16
SSM_GROUPS = SSM_WIDTH // SSM_GROUP
SSM_STATE = 64
D_FF = 256 * ((8 * D_MODEL // 3 + 255) // 256)
MIX_WIDTH = ATTN_WIDTH + CONV_WIDTH + SSM_WIDTH
EPS = 1e-6

OFF_K = ATTN_WIDTH
OFF_V = OFF_K + KV_WIDTH
OFF_CONV = OFF_V + KV_WIDTH
OFF_SSM = OFF_CONV + 2 * CONV_WIDTH
OFF_GATE = OFF_SSM + SSM_WIDTH

V7X_LANES = 128
V7X_SUBLANES = 8
V7X_VMEM_BYTES = 64 * 1024 * 1024
V7X_VMEM_UNREQUESTED_BYTES = 4 * 1024 * 1024
V7X_VMEM_INTERNAL_BYTES = 12 * 1024 * 1024

SSM_CHUNK = 8
SSM_QGROUPS = 8
SSM_QCH = SSM_QGROUPS * SSM_GROUP
SSM_NQ = SSM_GROUPS // SSM_QGROUPS
SSM_QSTATE = SSM_QGROUPS * SSM_STATE


def _nbytes(shape, dtype):
    return math.prod(shape) * jnp.dtype(dtype).itemsize


def _params(block_bytes, scratch_bytes=0, semantics=None, single_bytes=0):
    limit = 2 * block_bytes + single_bytes + scratch_bytes + V7X_VMEM_INTERNAL_BYTES
    limit = min(limit, V7X_VMEM_BYTES - V7X_VMEM_UNREQUESTED_BYTES)
    return pltpu.CompilerParams(dimension_semantics=semantics, vmem_limit_bytes=int(limit))


def _resident(shape, index_map):
    return pl.BlockSpec(shape, index_map, pipeline_mode=pl.Buffered(1))


def _weight(k, bn, layer, col):
    return pl.BlockSpec((None, k, bn), lambda i, j: (layer, 0, col(j)))


def _wdot(a, w_ref):
    return jnp.dot(a, w_ref[...].astype(BF16), preferred_element_type=F32)


ROW_SUB = 256


def _row_blocks(bm):
    sub = min(ROW_SUB, bm)
    return [slice(r, r + sub) for r in range(0, bm, sub)]


def _lane_partial_sq(y):
    sq = y * y
    part = sq[:, :V7X_LANES]
    for c in range(1, y.shape[1] // V7X_LANES):
        part = part + sq[:, c * V7X_LANES:(c + 1) * V7X_LANES]
    return part


def _row_scale(ssq_ref, d):
    return lax.rsqrt(jnp.sum(ssq_ref[...], axis=-1, keepdims=True) * (1.0 / d) + EPS)


def _norm_prep_kernel(x_ref, g_ref, xg_ref, ssq_ref):
    x = x_ref[...]
    xg_ref[...] = (x * g_ref[...]).astype(xg_ref.dtype)
    ssq_ref[...] = _lane_partial_sq(x)


def norm_prep(x, g, bm=256):
    m, d = x.shape
    blk = (_nbytes((bm, d), F32) + _nbytes((bm, d), BF16) + _nbytes((1, d), F32)
           + _nbytes((bm, V7X_LANES), F32))
    return pl.pallas_call(
        _norm_prep_kernel,
        grid=(m // bm,),
        in_specs=[pl.BlockSpec((bm, d), lambda i: (i, 0)),
                  pl.BlockSpec((1, d), lambda i: (0, 0))],
        out_specs=[pl.BlockSpec((bm, d), lambda i: (i, 0)),
                   pl.BlockSpec((bm, V7X_LANES), lambda i: (i, 0))],
        out_shape=[jax.ShapeDtypeStruct((m, d), BF16),
                   jax.ShapeDtypeStruct((m, V7X_LANES), F32)],
        compiler_params=_params(blk, semantics=("arbitrary",)),
        name="norm_prep",
    )(x, g.reshape(1, d))


def _swiglu_kernel(xg_ref, ssq_ref, wg_ref, wu_ref, o_ref, r_scr):
    @pl.when(pl.program_id(1) == 0)
    def _():
        r_scr[...] = _row_scale(ssq_ref, xg_ref.shape[1])

    wg = wg_ref[...].astype(BF16)
    wu = wu_ref[...].astype(BF16)
    for rows in _row_blocks(xg_ref.shape[0]):
        h = xg_ref[rows, :]
        r = r_scr[rows, :]
        g = r * jnp.dot(h, wg, preferred_element_type=F32)
        u = r * jnp.dot(h, wu, preferred_element_type=F32)
        o_ref[rows, :] = (g * jax.nn.sigmoid(g) * u).astype(o_ref.dtype)


def swiglu_in(xg, ssq, w, layer, bm=2048, bn=256):
    m, k = xg.shape
    f = w.shape[2] // 2
    bm = min(bm, m)
    nj = f // bn
    blk = 2 * _nbytes((k, bn), F32) + _nbytes((bm, bn), BF16)
    tmp = 2 * _nbytes((k, bn), BF16) + 2 * _nbytes((bm, bn), F32) + _nbytes((bm, V7X_LANES), F32)
    single = _nbytes((bm, k), BF16) + _nbytes((bm, V7X_LANES), F32)
    return pl.pallas_call(
        _swiglu_kernel,
        grid=(m // bm, nj),
        in_specs=[_resident((bm, k), lambda i, j: (i, 0)),
                  _resident((bm, V7X_LANES), lambda i, j: (i, 0)),
                  _weight(k, bn, layer, lambda j: j),
                  _weight(k, bn, layer, lambda j: j + nj)],
        out_specs=pl.BlockSpec((bm, bn), lambda i, j: (i, j)),
        out_shape=jax.ShapeDtypeStruct((m, f), BF16),
        scratch_shapes=[pltpu.VMEM((bm, 1), F32)],
        compiler_params=_params(blk, tmp, ("arbitrary", "arbitrary"), single),
        name="swiglu_in",
    )(xg, ssq, w, w)


def _matmul_kernel(xg_ref, ssq_ref, w_ref, o_ref, r_scr):
    @pl.when(pl.program_id(1) == 0)
    def _():
        r_scr[...] = _row_scale(ssq_ref, xg_ref.shape[1])

    w = w_ref[...].astype(BF16)
    for rows in _row_blocks(xg_ref.shape[0]):
        y = r_scr[rows, :] * jnp.dot(xg_ref[rows, :], w, preferred_element_type=F32)
        o_ref[rows, :] = y.astype(o_ref.dtype)


def matmul(xg, ssq, w, layer, out_dtype, bm=2048, bn=512):
    m, k = xg.shape
    n = w.shape[2]
    bm, bn = min(bm, m), min(bn, n)
    blk = _nbytes((k, bn), F32) + _nbytes((bm, bn), out_dtype)
    tmp = _nbytes((k, bn), BF16) + _nbytes((bm, bn), F32) + _nbytes((bm, V7X_LANES), F32)
    single = _nbytes((bm, k), BF16) + _nbytes((bm, V7X_LANES), F32)
    return pl.pallas_call(
        _matmul_kernel,
        grid=(m // bm, n // bn),
        in_specs=[_resident((bm, k), lambda i, j: (i, 0)),
                  _resident((bm, V7X_LANES), lambda i, j: (i, 0)),
                  _weight(k, bn, layer, lambda j: j)],
        out_specs=pl.BlockSpec((bm, bn), lambda i, j: (i, j)),
        out_shape=jax.ShapeDtypeStruct((m, n), out_dtype),
        scratch_shapes=[pltpu.VMEM((bm, 1), F32)],
        compiler_params=_params(blk, tmp, ("arbitrary", "arbitrary"), single),
        name="matmul",
    )(xg, ssq, w)


def _matmul_res_kernel(a_ref, w_ref, x_ref, o_ref, *, scale):
    w = w_ref[...].astype(BF16)
    for rows in _row_blocks(a_ref.shape[0]):
        o_ref[rows, :] = x_ref[rows, :] + scale * jnp.dot(a_ref[rows, :], w, preferred_element_type=F32)


def _matmul_res_norm_kernel(a_ref, w_ref, x_ref, g_ref, o_ref, xg_ref, ssq_ref, *, scale):
    @pl.when(pl.program_id(1) == 0)
    def _():
        ssq_ref[...] = jnp.zeros(ssq_ref.shape, F32)

    w = w_ref[...].astype(BF16)
    for rows in _row_blocks(a_ref.shape[0]):
        y = x_ref[rows, :] + scale * jnp.dot(a_ref[rows, :], w, preferred_element_type=F32)
        o_ref[rows, :] = y
        xg_ref[rows, :] = (y * g_ref[...]).astype(xg_ref.dtype)
        ssq_ref[rows, :] += _lane_partial_sq(y)


def matmul_res(a, w, layer, x, scale, bm, next_gain=None, bn=256):
    m, k = a.shape
    n = w.shape[2]
    bm, bn = min(bm, m), min(bn, n)
    tile = pl.BlockSpec((bm, bn), lambda i, j: (i, j))
    in_specs = [_resident((bm, k), lambda i, j: (i, 0)), _weight(k, bn, layer, lambda j: j), tile]
    blk = _nbytes((k, bn), F32) + 2 * _nbytes((bm, bn), F32)
    tmp = _nbytes((k, bn), BF16) + _nbytes((bm, bn), F32)
    single = _nbytes((bm, k), a.dtype)
    sem = ("arbitrary", "arbitrary")
    if next_gain is None:
        return pl.pallas_call(
            functools.partial(_matmul_res_kernel, scale=scale),
            grid=(m // bm, n // bn),
            in_specs=in_specs,
            out_specs=tile,
            out_shape=jax.ShapeDtypeStruct((m, n), F32),
            compiler_params=_params(blk, tmp, sem, single),
            name="matmul_res",
        )(a, w, x)
    blk += _nbytes((1, bn), F32) + _nbytes((bm, bn), BF16) + _nbytes((bm, V7X_LANES), F32)
    return pl.pallas_call(
        functools.partial(_matmul_res_norm_kernel, scale=scale),
        grid=(m // bm, n // bn),
        in_specs=in_specs + [pl.BlockSpec((1, bn), lambda i, j: (0, j))],
        out_specs=[tile, tile, pl.BlockSpec((bm, V7X_LANES), lambda i, j: (i, 0))],
        out_shape=[jax.ShapeDtypeStruct((m, n), F32), jax.ShapeDtypeStruct((m, n), BF16),
                   jax.ShapeDtypeStruct((m, V7X_LANES), F32)],
        compiler_params=_params(blk, tmp, sem, single),
        name="matmul_res_norm",
    )(a, w, x, next_gain.astype(F32).reshape(1, n))


def _glu_kernel(a_ref, wa_ref, wb_ref, o_ref):
    wa = wa_ref[...].astype(BF16)
    wb = wb_ref[...].astype(BF16)
    for rows in _row_blocks(a_ref.shape[0]):
        a = a_ref[rows, :].astype(BF16)
        za = jnp.dot(a, wa, preferred_element_type=F32)
        zb = jnp.dot(a, wb, preferred_element_type=F32)
        o_ref[rows, :] = (za * jax.nn.sigmoid(zb)).astype(o_ref.dtype)


def glu_matmul(a, w, layer, bm=2048, bn=256):
    m, k = a.shape
    n = w.shape[2] // 2
    bm = min(bm, m)
    nj = n // bn
    blk = 2 * _nbytes((k, bn), F32) + _nbytes((bm, bn), BF16)
    tmp = _nbytes((bm, k), BF16) + 2 * _nbytes((k, bn), BF16) + 2 * _nbytes((bm, bn), F32)
    return pl.pallas_call(
        _glu_kernel,
        grid=(m // bm, nj),
        in_specs=[_resident((bm, k), lambda i, j: (i, 0)),
                  _weight(k, bn, layer, lambda j: j),
                  _weight(k, bn, layer, lambda j: j + nj)],
        out_specs=pl.BlockSpec((bm, bn), lambda i, j: (i, j)),
        out_shape=jax.ShapeDtypeStruct((m, n), BF16),
        compiler_params=_params(blk, tmp, ("arbitrary", "arbitrary"), _nbytes((bm, k), a.dtype)),
        name="glu_matmul",
    )(a, w, w)


LOG2E = math.log2(math.e)


def _segment_mean_sq(x, seg_ref):
    sq = x * x
    hi = sq.astype(BF16)
    lo = (sq - hi.astype(F32)).astype(BF16)
    s = (jnp.dot(hi, seg_ref[...], preferred_element_type=F32)
         + jnp.dot(lo, seg_ref[...], preferred_element_type=F32))
    return s * (1.0 / HEAD_DIM)


def _attn_kernel(sink_ref, q_ref, kp_ref, kc_ref, vp_ref, vc_ref, bias_ref, qg_ref, kg_ref,
                 seg_ref, o_ref, s_scr, p_scr, sink_scr):
    lanes = V7X_LANES
    low = lax.broadcasted_iota(jnp.int32, (1, lanes), 1) < HEAD_DIM
    high = jnp.logical_not(low)

    kk = jnp.concatenate([kp_ref[...], kc_ref[...]], axis=0)
    vv = jnp.concatenate([vp_ref[...], vc_ref[...]], axis=0)
    ones = jnp.ones((HEAD_DIM, 2 * BLOCK), F32)

    k_low, k_high, v_low, v_high = [], [], [], []
    for c in range(KV_WIDTH // lanes):
        kc = kk[:, c * lanes:(c + 1) * lanes]
        kc = kc * lax.rsqrt(_segment_mean_sq(kc, seg_ref) + EPS) * kg_ref[:, c * lanes:(c + 1) * lanes]
        ks = pltpu.roll(kc, HEAD_DIM, axis=1)
        vt = vv[:, c * lanes:(c + 1) * lanes].T
        for odd in range(2):
            k_low.append(jnp.where(low, ks if odd else kc, 0.0).astype(BF16))
            k_high.append(jnp.where(high, kc if odd else ks, 0.0).astype(BF16))
            vth = vt[odd * HEAD_DIM:(odd + 1) * HEAD_DIM, :]
            v_low.append(jnp.concatenate([vth, ones], axis=0).astype(BF16))
            v_high.append(jnp.concatenate([ones, vth], axis=0).astype(BF16))

    n_pairs = ATTN_WIDTH // lanes
    for pair in range(n_pairs):
        kv = pair // (Q_GROUP // 2)
        qt = q_ref[:, pair * lanes:(pair + 1) * lanes].T
        ms = [jnp.mean(jnp.square(qt[h * HEAD_DIM:(h + 1) * HEAD_DIM]), axis=0, keepdims=True)
              for h in range(2)]
        rs = jnp.concatenate([jnp.broadcast_to(lax.rsqrt(m + EPS), (HEAD_DIM, BLOCK)) for m in ms], axis=0)
        qt = (qt * rs * qg_ref[...]).astype(BF16)
        for half, kh in enumerate((k_low[kv], k_high[kv])):
            head = 2 * pair + half
            s_scr[head] = jnp.dot(kh, qt, preferred_element_type=F32) + bias_ref[head]

    for head in range(N_Q_HEADS):
        s = s_scr[head]
        sink = sink_ref[head]
        mx = jnp.maximum(jnp.max(s, axis=0, keepdims=True), sink)
        p_scr[head] = jnp.exp2(s - mx).astype(BF16)
        sink_scr[head:head + 1, :] = jnp.exp2(sink - mx)

    for pair in range(n_pairs):
        kv = pair // (Q_GROUP // 2)
        num, den = [], []
        for half, vh in enumerate((v_low[kv], v_high[kv])):
            head = 2 * pair + half
            o = jnp.dot(vh, p_scr[head], preferred_element_type=F32)
            lo_rows, hi_rows = o[:HEAD_DIM], o[HEAD_DIM:]
            num.append(lo_rows if half == 0 else hi_rows)
            den.append((hi_rows if half == 0 else lo_rows) + sink_scr[head:head + 1, :])
        ot = jnp.concatenate(num, axis=0) / jnp.concatenate(den, axis=0)
        o_ref[:, pair * lanes:(pair + 1) * lanes] = ot.T.astype(o_ref.dtype)


def attention(proj, bias, q_gain, k_gain, sinks, seq):
    m = proj.shape[0]
    nb = seq // BLOCK
    nblk = m // BLOCK
    lanes = V7X_LANES
    seg_id = jnp.arange(lanes) // HEAD_DIM
    seg = (seg_id[:, None] == seg_id[None, :]).astype(BF16)
    qg = jnp.broadcast_to(jnp.tile(q_gain.astype(F32) * (HEAD_DIM ** -0.5 * LOG2E), 2)[:, None], (lanes, BLOCK))
    kg = jnp.tile(k_gain.astype(F32), N_KV_HEADS).reshape(1, KV_WIDTH)
    kcol, vcol = OFF_K // KV_WIDTH, OFF_V // KV_WIDTH
    table = (None,) + bias.shape[1:]

    def prev(i):
        return jnp.where(i % nb == 0, i, i - 1)

    blk = (_nbytes((BLOCK, ATTN_WIDTH), F32) + 4 * _nbytes((BLOCK, KV_WIDTH), F32)
           + _nbytes(bias.shape[1:], F32) + _nbytes((BLOCK, ATTN_WIDTH), BF16))
    scratch = (_nbytes((N_Q_HEADS, 2 * BLOCK, BLOCK), F32) + _nbytes((N_Q_HEADS, 2 * BLOCK, BLOCK), BF16)
               + _nbytes((N_Q_HEADS, BLOCK), F32))
    return pl.pallas_call(
        _attn_kernel,
        grid=(nblk,),
        in_specs=[pl.BlockSpec(memory_space=pltpu.SMEM),
                  pl.BlockSpec((BLOCK, ATTN_WIDTH), lambda i: (i, 0)),
                  pl.BlockSpec((BLOCK, KV_WIDTH), lambda i: (prev(i), kcol)),
                  pl.BlockSpec((BLOCK, KV_WIDTH), lambda i: (i, kcol)),
                  pl.BlockSpec((BLOCK, KV_WIDTH), lambda i: (prev(i), vcol)),
                  pl.BlockSpec((BLOCK, KV_WIDTH), lambda i: (i, vcol)),
                  pl.BlockSpec(table, lambda i: (jnp.where(i % nb == 0, 0, 1), 0, 0, 0)),
                  pl.BlockSpec((lanes, BLOCK), lambda i: (0, 0)),
                  pl.BlockSpec((1, KV_WIDTH), lambda i: (0, 0)),
                  pl.BlockSpec((lanes, lanes), lambda i: (0, 0))],
        out_specs=pl.BlockSpec((BLOCK, ATTN_WIDTH), lambda i: (i, 0)),
        out_shape=jax.ShapeDtypeStruct((m, ATTN_WIDTH), BF16),
        scratch_shapes=[pltpu.VMEM((N_Q_HEADS, 2 * BLOCK, BLOCK), F32),
                        pltpu.VMEM((N_Q_HEADS, 2 * BLOCK, BLOCK), BF16),
                        pltpu.VMEM((N_Q_HEADS, BLOCK), F32)],
        compiler_params=_params(blk, scratch, semantics=("arbitrary",)),
        name="swa_attention",
    )(sinks.astype(F32) * LOG2E, proj, proj, proj, proj, proj, bias, qg, kg, seg)


def _t5_bucket(dist):
    max_exact = N_BUCKETS // 2
    n = jnp.maximum(dist, 0)
    ratio = jnp.log(jnp.maximum(n, 1).astype(F32) / max_exact) / math.log(MAX_DISTANCE / max_exact)
    large = jnp.minimum(max_exact + (ratio * (N_BUCKETS - max_exact)).astype(jnp.int32), N_BUCKETS - 1)
    return jnp.where(n < max_exact, n, large)


def band_bias(rel_bias):
    keys, period = 2 * BLOCK, 4 * BLOCK
    by_dist = rel_bias.astype(F32)[_t5_bucket(jnp.arange(WINDOW))].T * LOG2E
    v = jnp.concatenate([by_dist, jnp.full((N_Q_HEADS, period - WINDOW), -jnp.inf, F32)], axis=1)
    a = jnp.tile(v, (1, keys))[:, :keys * (period - 1)].reshape(N_Q_HEADS, keys, period - 1)
    rest = a[:, :, BLOCK:2 * BLOCK]
    c = jnp.arange(keys)[None, :, None]
    first = jnp.where(c >= BLOCK, rest, -jnp.inf)
    return jnp.stack([first, rest])


CONV_HALO = 32
CONV_ROWS = 32


def _conv_kernel(a_ref, g_ref, w_ref, b_ref, lg_ref, lb_ref, pw_ref, o_ref, ubuf, shbuf, ybuf, *, nt, tt):
    t = pl.program_id(0) % nt

    @pl.when(t == 0)
    def _():
        ubuf[0:CONV_HALO, :] = jnp.zeros((CONV_HALO, CONV_WIDTH), F32)

    @pl.when(t != 0)
    def _():
        ubuf[0:CONV_HALO, :] = ubuf[tt:tt + CONV_HALO, :]

    ubuf[CONV_HALO:CONV_HALO + tt, :] = a_ref[...] * jax.nn.sigmoid(g_ref[...])

    first = CONV_HALO - (CONV_KERNEL - 1)
    cw = 2 * V7X_LANES

    span = tt + CONV_HALO - V7X_SUBLANES
    for r in range(1, V7X_SUBLANES):
        shbuf[r - 1, 0:span, :] = ubuf[r:r + span, :]

    def rows(rb, carry):
        base = pl.multiple_of(rb * CONV_ROWS, CONV_ROWS)
        for cb in range(CONV_WIDTH // cw):
            cols = slice(cb * cw, (cb + 1) * cw)
            acc = jnp.zeros((CONV_ROWS, cw), F32) + b_ref[:, cols]
            for j in range(CONV_KERNEL):
                r, a = (first + j) % V7X_SUBLANES, (first + j) // V7X_SUBLANES * V7X_SUBLANES
                tap = (ubuf[pl.ds(base + a, CONV_ROWS), cols] if r == 0
                       else shbuf[r - 1, pl.ds(base + a, CONV_ROWS), cols])
                acc = acc + w_ref[j:j + 1, cols] * tap
            ybuf[pl.ds(base, CONV_ROWS), cols] = acc
        return carry

    lax.fori_loop(0, tt // CONV_ROWS, rows, 0)

    y = ybuf[...]
    mu = jnp.mean(y, axis=-1, keepdims=True)
    yc = y - mu
    var = jnp.mean(yc * yc, axis=-1, keepdims=True)
    z = yc * lax.rsqrt(var + EPS) * lg_ref[...] + lb_ref[...]
    z = z * jax.nn.sigmoid(z)
    o_ref[...] = _wdot(z.astype(BF16), pw_ref).astype(o_ref.dtype)


def conv_branch(proj, w_dw, b_dw, ln_g, ln_b, w_pw, layer, seq, tt=512):
    m = proj.shape[0]
    tt = min(tt, seq)
    nt = seq // tt
    acol = OFF_CONV // CONV_WIDTH
    w_pad = jnp.zeros((CONV_HALO, CONV_WIDTH), F32).at[:CONV_KERNEL].set(w_dw.astype(F32))
    row = lambda v: v.astype(F32).reshape(1, CONV_WIDTH)
    blk = (2 * _nbytes((tt, CONV_WIDTH), F32) + _nbytes((CONV_HALO, CONV_WIDTH), F32)
           + 3 * _nbytes((1, CONV_WIDTH), F32) + _nbytes((CONV_WIDTH, CONV_WIDTH), F32)
           + _nbytes((tt, CONV_WIDTH), BF16))
    scratch = V7X_SUBLANES * _nbytes((tt + CONV_HALO, CONV_WIDTH), F32) + _nbytes((tt, CONV_WIDTH), F32)
    return pl.pallas_call(
        functools.partial(_conv_kernel, nt=nt, tt=tt),
        grid=(m // tt,),
        in_specs=[pl.BlockSpec((tt, CONV_WIDTH), lambda i: (i, acol)),
                  pl.BlockSpec((tt, CONV_WIDTH), lambda i: (i, acol + 1)),
                  pl.BlockSpec((CONV_HALO, CONV_WIDTH), lambda i: (0, 0)),
                  pl.BlockSpec((1, CONV_WIDTH), lambda i: (0, 0)),
                  pl.BlockSpec((1, CONV_WIDTH), lambda i: (0, 0)),
                  pl.BlockSpec((1, CONV_WIDTH), lambda i: (0, 0)),
                  pl.BlockSpec((None, CONV_WIDTH, CONV_WIDTH), lambda i: (layer, 0, 0))],
        out_specs=pl.BlockSpec((tt, CONV_WIDTH), lambda i: (i, 0)),
        out_shape=jax.ShapeDtypeStruct((m, CONV_WIDTH), BF16),
        scratch_shapes=[pltpu.VMEM((tt + CONV_HALO, CONV_WIDTH), F32),
                        pltpu.VMEM((V7X_SUBLANES - 1, tt + CONV_HALO, CONV_WIDTH), F32),
                        pltpu.VMEM((tt, CONV_WIDTH), F32)],
        compiler_params=_params(blk, scratch, semantics=("arbitrary",)),
        name="conv_branch",
    )(proj, proj, w_pad, row(b_dw), row(ln_g), row(ln_b), w_pw)


def _swap_re_im(h):
    lanes = V7X_LANES
    return jnp.concatenate([pltpu.roll(h[:, k * lanes:(k + 1) * lanes], SSM_STATE, axis=1)
                            for k in range(h.shape[1] // lanes)], axis=1)


def _expand_operands(e_ref, f_ref, ws_scr, wi_scr, wxt_scr):
    T, rows_q, lanes = SSM_CHUNK, SSM_QCH, V7X_LANES
    wide = SSM_QGROUPS * lanes
    r = lax.broadcasted_iota(jnp.int32, (rows_q, wide), 0)
    c = lax.broadcasted_iota(jnp.int32, (rows_q, wide), 1)
    own_slab = (r // SSM_GROUP) == (c // lanes)
    r2 = lax.broadcasted_iota(jnp.int32, (rows_q, rows_q), 0)
    c2 = lax.broadcasted_iota(jnp.int32, (rows_q, rows_q), 1)
    same_group = (r2 // SSM_GROUP) == (c2 // SSM_GROUP)

    def group_diag(blk):
        return jnp.where(own_slab, jnp.concatenate([blk] * SSM_QGROUPS, axis=1), 0.0).astype(BF16)

    for s in range(T):
        ws_scr[s * rows_q:(s + 1) * rows_q, :] = group_diag(e_ref[0, T - 1 - s])
        wxt_scr[s * rows_q:(s + 1) * rows_q, :] = group_diag(f_ref[0, s + 1])
    f0 = f_ref[0, 0]
    zero = jnp.zeros((rows_q, rows_q), BF16)
    kbd = []
    for l in range(T):
        k_l = lax.dot_general(e_ref[0, l], f0, (((1,), (1,)), ((), ())), preferred_element_type=F32,
                              precision=lax.Precision.HIGHEST)
        kbd.append(jnp.where(same_group, k_l, 0.0).astype(BF16))
    for s in range(T):
        for t in range(T):
            wi_scr[s * rows_q:(s + 1) * rows_q, t * rows_q:(t + 1) * rows_q] = kbd[t - s] if t >= s else zero


def _ssm_kernel(u_ref, e_ref, f_ref, a1_ref, a2_ref, d_ref, o_ref, ws_scr, wi_scr, wxt_scr, *, nc):
    @pl.when(pl.program_id(1) == 0)
    def _():
        _expand_operands(e_ref, f_ref, ws_scr, wi_scr, wxt_scr)

    x = jnp.concatenate([u_ref[pl.ds(t, nc, stride=SSM_CHUNK), :] for t in range(SSM_CHUNK)], axis=1)
    xb = x.astype(BF16)
    h = jnp.dot(xb, ws_scr[...], preferred_element_type=F32)
    row = lax.broadcasted_iota(jnp.int32, (nc, 1), 0)
    level = 0
    while (1 << level) < nc:
        d = 1 << level
        a1 = a1_ref[0, level:level + 1, :]
        a2 = a2_ref[0, level:level + 1, :]
        if d % V7X_SUBLANES:
            prev = jnp.where(row >= d, pltpu.roll(h, d, axis=0), 0.0)
            h = h + a1 * prev + a2 * _swap_re_im(prev)
        else:
            prev = h[:nc - d]
            h = jnp.concatenate([h[:d], h[d:] + a1 * prev + a2 * _swap_re_im(prev)], axis=0)
        level += 1
    hp = jnp.where(row >= 1, pltpu.roll(h, 1, axis=0), 0.0).astype(BF16)
    y = (jnp.dot(xb, wi_scr[...], preferred_element_type=F32)
         + lax.dot_general(hp, wxt_scr[...], (((1,), (1,)), ((), ())), preferred_element_type=F32)
         + d_ref[0] * x)
    y = jax.nn.gelu(y)
    for t in range(SSM_CHUNK):
        o_ref[pl.ds(t, nc, stride=SSM_CHUNK), :] = y[:, t * SSM_QCH:(t + 1) * SSM_QCH]


def ssm_branch(proj, operands, d_skip, layer, seq):
    e_blk, f_blk, a1, a2 = operands
    m = proj.shape[0]
    nseq = m // seq
    nc = seq // SSM_CHUNK
    ucol = OFF_SSM // SSM_QCH
    kw = SSM_CHUNK * SSM_QCH
    sw = 2 * SSM_QSTATE
    nlev = a1.shape[2]
    d_t = jnp.tile(d_skip.astype(F32).reshape(-1, SSM_NQ, 1, SSM_QCH), (1, 1, 1, SSM_CHUNK))
    blk = (2 * _nbytes((seq, SSM_QCH), F32) + _nbytes(e_blk.shape[2:], F32) + _nbytes(f_blk.shape[2:], F32)
           + 2 * _nbytes((nlev, sw), F32) + _nbytes((1, kw), F32))
    scratch = 2 * _nbytes((kw, sw), BF16) + _nbytes((kw, kw), BF16)
    tmp = 6 * _nbytes((nc, kw), F32)

    def per_q(shape):
        return pl.BlockSpec((None, 1) + shape, lambda q, b: (layer, q) + (0,) * len(shape))

    return pl.pallas_call(
        functools.partial(_ssm_kernel, nc=nc),
        grid=(SSM_NQ, nseq),
        in_specs=[pl.BlockSpec((seq, SSM_QCH), lambda q, b: (b, ucol + q)),
                  per_q(e_blk.shape[2:]), per_q(f_blk.shape[2:]),
                  per_q((nlev, sw)), per_q((nlev, sw)), per_q((1, kw))],
        out_specs=pl.BlockSpec((seq, SSM_QCH), lambda q, b: (b, q)),
        out_shape=jax.ShapeDtypeStruct((m, SSM_WIDTH), F32),
        scratch_shapes=[pltpu.VMEM((kw, sw), BF16), pltpu.VMEM((kw, kw), BF16), pltpu.VMEM((kw, sw), BF16)],
        compiler_params=_params(blk, scratch + tmp, semantics=("arbitrary", "arbitrary")),
        name="ssm_scan",
    )(proj, e_blk, f_blk, a1, a2, d_t)


def _cmul(ar, ai, br, bi):
    return ar * br - ai * bi, ar * bi + ai * br


def ssm_operands(a_re, a_im, log_dt, b_re, b_im, c_re, c_im, n_chunks):
    f32 = F32
    dt = jnp.exp(log_dt.astype(f32))[:, None]
    lam_re, lam_im = a_re.astype(f32), a_im.astype(f32)
    mag = jnp.exp(dt * lam_re)
    ang = dt * lam_im
    lb_re, lb_im = mag * jnp.cos(ang), mag * jnp.sin(ang)
    nr = lb_re - 1.0
    den = lam_re * lam_re + lam_im * lam_im
    coef_re = (nr * lam_re + lb_im * lam_im) / den
    coef_im = (lb_im * lam_re - nr * lam_im) / den
    br, bi = b_re.astype(f32), b_im.astype(f32)
    bb_re = coef_re[..., None] * br - coef_im[..., None] * bi
    bb_im = coef_re[..., None] * bi + coef_im[..., None] * br
    cr, ci = c_re.astype(f32), c_im.astype(f32)

    pw = [(jnp.ones_like(lb_re), jnp.zeros_like(lb_im))]
    for _ in range(SSM_CHUNK):
        pw.append(_cmul(pw[-1][0], pw[-1][1], lb_re, lb_im))

    T, Q, C, P = SSM_CHUNK, SSM_QGROUPS, SSM_GROUP, SSM_STATE
    rows_q = Q * C
    pw_re = jnp.stack([p[0] for p in pw])
    pw_im = jnp.stack([p[1] for p in pw])

    def per_tile(blocks):
        k = blocks.shape[0]
        return blocks.reshape(k, SSM_NQ, rows_q, 2 * P).transpose(1, 0, 2, 3)

    e_re, e_im = _cmul(pw_re[:T, :, :, None], pw_im[:T, :, :, None], bb_re, bb_im)
    e_blk = per_tile(jnp.concatenate([jnp.swapaxes(e_re, 2, 3), jnp.swapaxes(e_im, 2, 3)], axis=-1))
    f_re, f_im = _cmul(cr, ci, pw_re[:, :, None, :], pw_im[:, :, None, :])
    f_blk = per_tile(jnp.concatenate([f_re, -f_im], axis=-1))

    a = pw[T]
    a1, a2 = [], []
    level = 0
    while (1 << level) < n_chunks:
        a1.append(jnp.concatenate([a[0], a[0]], axis=-1).reshape(SSM_NQ, Q * 2 * P))
        a2.append(jnp.concatenate([-a[1], a[1]], axis=-1).reshape(SSM_NQ, Q * 2 * P))
        a = _cmul(a[0], a[1], a[0], a[1])
        level += 1
    return e_blk, f_blk, jnp.stack(a1, axis=1), jnp.stack(a2, axis=1)


def _merge_kernel(oa_ref, oc_ref, os_ref, ga_ref, gc_ref, gs_ref, w_ref, o_ref):
    wa = w_ref[0:ATTN_WIDTH, :].astype(BF16)
    wc = w_ref[ATTN_WIDTH:ATTN_WIDTH + CONV_WIDTH, :].astype(BF16)
    ws = w_ref[ATTN_WIDTH + CONV_WIDTH:MIX_WIDTH, :].astype(BF16)
    for rows in _row_blocks(o_ref.shape[0]):
        ya = jnp.dot(oa_ref[rows, :], wa, preferred_element_type=F32)
        yc = jnp.dot(oc_ref[rows, :], wc, preferred_element_type=F32)
        ys = jnp.dot(os_ref[rows, :], ws, preferred_element_type=F32)
        merged = (jax.nn.sigmoid(ga_ref[rows, :]) * ya + jax.nn.sigmoid(gc_ref[rows, :]) * yc
                  + jax.nn.sigmoid(gs_ref[rows, :]) * ys)
        o_ref[rows, :] = merged.astype(o_ref.dtype)


def branch_merge(o_attn, o_conv, o_ssm, proj, w_branch, layer, bm=2048, bn=256):
    m = proj.shape[0]
    bm = min(bm, m)
    gcol = OFF_GATE // bn
    gstep = D_MODEL // bn
    blk = 3 * _nbytes((bm, bn), F32) + _nbytes((MIX_WIDTH, bn), F32) + _nbytes((bm, bn), BF16)
    tmp = _nbytes((MIX_WIDTH, bn), BF16) + 4 * _nbytes((bm, bn), F32)
    return pl.pallas_call(
        _merge_kernel,
        grid=(m // bm, D_MODEL // bn),
        in_specs=[_resident((bm, ATTN_WIDTH), lambda i, j: (i, 0)),
                  _resident((bm, CONV_WIDTH), lambda i, j: (i, 0)),
                  _resident((bm, SSM_WIDTH), lambda i, j: (i, 0)),
                  pl.BlockSpec((bm, bn), lambda i, j: (i, gcol + j)),
                  pl.BlockSpec((bm, bn), lambda i, j: (i, gcol + gstep + j)),
                  pl.BlockSpec((bm, bn), lambda i, j: (i, gcol + 2 * gstep + j)),
                  _weight(MIX_WIDTH, bn, layer, lambda j: j)],
        out_specs=pl.BlockSpec((bm, bn), lambda i, j: (i, j)),
        out_shape=jax.ShapeDtypeStruct((m, D_MODEL), BF16),
        compiler_params=_params(blk, tmp, ("arbitrary", "arbitrary"), _nbytes((bm, MIX_WIDTH), BF16)),
        name="branch_merge",
    )(o_attn, o_conv, o_ssm, proj, proj, proj, w_branch)


FFN_OUT_ROWS = 1024


def _ffn(x, normed, w_in, w_out, layer, next_gain):
    act = swiglu_in(*normed, w_in, layer)
    return matmul_res(act, w_out, layer, x, 0.5, bm=FFN_OUT_ROWS, next_gain=next_gain)


def _layer(x, normed, seq, bias, ssm_ops, p, layer):
    at = lambda name: p[name][layer]
    x, *normed = _ffn(x, normed, p['w_ffn1_in'], p['w_ffn1_out'], layer, at('mix_norm'))
    proj = matmul(*normed, p['w_in'], layer, F32)
    o_attn = attention(proj, bias, at('q_norm'), at('k_norm'), at('attn_sinks'), seq)
    o_conv = conv_branch(proj, at('conv_dw'), at('conv_dw_bias'), at('conv_ln_g'), at('conv_ln_b'),
                         p['conv_pw'], layer, seq)
    y_ssm = ssm_branch(proj, ssm_ops, p['ssm_d'], layer, seq)
    o_ssm = glu_matmul(y_ssm, p['ssm_glu'], layer)
    merged = branch_merge(o_attn, o_conv, o_ssm, proj, p['w_branch'], layer)
    x, *normed = matmul_res(merged, p['w_out'], layer, x, 1.0, bm=2048, next_gain=at('ffn2_norm'))
    if layer + 1 == DEPTH:
        return _ffn(x, normed, p['w_ffn2_in'], p['w_ffn2_out'], layer, None), None
    x, *normed = _ffn(x, normed, p['w_ffn2_in'], p['w_ffn2_out'], layer, p['ffn1_norm'][layer + 1])
    return x, normed


def kernel(x, rel_bias, ffn1_norm, w_ffn1_in, w_ffn1_out, mix_norm, w_in, q_norm, k_norm, attn_sinks,
           conv_dw, conv_dw_bias, conv_ln_g, conv_ln_b, conv_pw, ssm_a_re, ssm_a_im, ssm_log_dt,
           ssm_b_re, ssm_b_im, ssm_c_re, ssm_c_im, ssm_d, ssm_glu, w_branch, w_out, ffn2_norm,
           w_ffn2_in, w_ffn2_out):
    layer_params = dict(
        ffn1_norm=ffn1_norm, w_ffn1_in=w_ffn1_in, w_ffn1_out=w_ffn1_out, mix_norm=mix_norm, w_in=w_in,
        q_norm=q_norm, k_norm=k_norm, attn_sinks=attn_sinks, conv_dw=conv_dw, conv_dw_bias=conv_dw_bias,
        conv_ln_g=conv_ln_g, conv_ln_b=conv_ln_b, conv_pw=conv_pw, ssm_a_re=ssm_a_re, ssm_a_im=ssm_a_im,
        ssm_log_dt=ssm_log_dt, ssm_b_re=ssm_b_re, ssm_b_im=ssm_b_im, ssm_c_re=ssm_c_re, ssm_c_im=ssm_c_im,
        ssm_d=ssm_d, ssm_glu=ssm_glu, w_branch=w_branch, w_out=w_out, ffn2_norm=ffn2_norm,
        w_ffn2_in=w_ffn2_in, w_ffn2_out=w_ffn2_out)
    b, seq, d = x.shape
    bias = band_bias(rel_bias)
    ssm_ops = jax.vmap(functools.partial(ssm_operands, n_chunks=seq // SSM_CHUNK))(
        ssm_a_re, ssm_a_im, ssm_log_dt, ssm_b_re, ssm_b_im, ssm_c_re, ssm_c_im)
    xf = x.reshape(b * seq, d)
    normed = norm_prep(xf, ffn1_norm[0])
    for layer in range(DEPTH):
        xf, normed = _layer(xf, normed, seq, bias, ssm_ops, layer_params, layer)
    return xf.reshape(b, seq, d)
```

```python
import functools
import math

import jax
import jax.numpy as jnp
from jax import lax
from jax.experimental import pallas as pl
from jax.experimental.pallas import tpu as pltpu

F32 = jnp.float32
BF16 = jnp.bfloat16

D_MODEL = 4096
DEPTH = 2
N_Q_HEADS = 32
N_KV_HEADS = 8
HEAD_DIM = 64
Q_GROUP = N_Q_HEADS // N_KV_HEADS
ATTN_WIDTH = N_Q_HEADS * HEAD_DIM
KV_WIDTH = N_KV_HEADS * HEAD_DIM
WINDOW = 128
BLOCK = 128
N_BUCKETS = 32
MAX_DISTANCE = 128
CONV_WIDTH = D_MODEL // 4
CONV_KERNEL = 31
SSM_WIDTH = D_MODEL // 4
SSM_GROUP = 16
SSM_GROUPS = SSM_WIDTH // SSM_GROUP
SSM_STATE = 64
D_FF = 256 * ((8 * D_MODEL // 3 + 255) // 256)
MIX_WIDTH = ATTN_WIDTH + CONV_WIDTH + SSM_WIDTH
EPS = 1e-6

OFF_K = ATTN_WIDTH
OFF_V = OFF_K + KV_WIDTH
OFF_CONV = OFF_V + KV_WIDTH
OFF_SSM = OFF_CONV + 2 * CONV_WIDTH
OFF_GATE = OFF_SSM + SSM_WIDTH

V7X_LANES = 128
V7X_SUBLANES = 8
V7X_VMEM_BYTES = 64 * 1024 * 1024
V7X_VMEM_UNREQUESTED_BYTES = 4 * 1024 * 1024
V7X_VMEM_INTERNAL_BYTES = 12 * 1024 * 1024

SSM_CHUNK = 8
SSM_QGROUPS = 8
SSM_QCH = SSM_QGROUPS * SSM_GROUP
SSM_NQ = SSM_GROUPS // SSM_QGROUPS
SSM_QSTATE = SSM_QGROUPS * SSM_STATE


def _nbytes(shape, dtype):
    return math.prod(shape) * jnp.dtype(dtype).itemsize


def _params(block_bytes, scratch_bytes=0, semantics=None, single_bytes=0):
    limit = 2 * block_bytes + single_bytes + scratch_bytes + V7X_VMEM_INTERNAL_BYTES
    limit = min(limit, V7X_VMEM_BYTES - V7X_VMEM_UNREQUESTED_BYTES)
    return pltpu.CompilerParams(dimension_semantics=semantics, vmem_limit_bytes=int(limit))


def _resident(shape, index_map):
    return pl.BlockSpec(shape, index_map, pipeline_mode=pl.Buffered(1))


def _weight(k, bn, layer, col):
    return pl.BlockSpec((None, k, bn), lambda i, j: (layer, 0, col(j)))


def _wdot(a, w_ref):
    return jnp.dot(a, w_ref[...].astype(BF16), preferred_element_type=F32)


ROW_SUB = 256


def _row_blocks(bm):
    sub = min(ROW_SUB, bm)
    return [slice(r, r + sub) for r in range(0, bm, sub)]


def _lane_partial_sq(y):
    sq = y * y
    part = sq[:, :V7X_LANES]
    for c in range(1, y.shape[1] // V7X_LANES):
        part = part + sq[:, c * V7X_LANES:(c + 1) * V7X_LANES]
    return part


def _row_scale(ssq_ref, d):
    return lax.rsqrt(jnp.sum(ssq_ref[...], axis=-1, keepdims=True) * (1.0 / d) + EPS)


def _norm_prep_kernel(x_ref, g_ref, xg_ref, ssq_ref):
    x = x_ref[...]
    xg_ref[...] = (x * g_ref[...]).astype(xg_ref.dtype)
    ssq_ref[...] = _lane_partial_sq(x)


def norm_prep(x, g, bm=256):
    m, d = x.shape
    blk = (_nbytes((bm, d), F32) + _nbytes((bm, d), BF16) + _nbytes((1, d), F32)
           + _nbytes((bm, V7X_LANES), F32))
    return pl.pallas_call(
        _norm_prep_kernel,
        grid=(m // bm,),
        in_specs=[pl.BlockSpec((bm, d), lambda i: (i, 0)),
                  pl.BlockSpec((1, d), lambda i: (0, 0))],
        out_specs=[pl.BlockSpec((bm, d), lambda i: (i, 0)),
                   pl.BlockSpec((bm, V7X_LANES), lambda i: (i, 0))],
        out_shape=[jax.ShapeDtypeStruct((m, d), BF16),
                   jax.ShapeDtypeStruct((m, V7X_LANES), F32)],
        compiler_params=_params(blk, semantics=("arbitrary",)),
        name="norm_prep",
    )(x, g.reshape(1, d))


def _swiglu_kernel(xg_ref, ssq_ref, wg_ref, wu_ref, o_ref, wg_scr, wu_scr):
    @pl.when(pl.program_id(1) == 0)
    def _():
        wg_scr[...] = wg_ref[...].astype(BF16)
        wu_scr[...] = wu_ref[...].astype(BF16)

    for rows in _row_blocks(xg_ref.shape[0]):
        h = xg_ref[rows, :]
        r = lax.rsqrt(jnp.sum(ssq_ref[rows, :], axis=-1, keepdims=True) * (1.0 / xg_ref.shape[1]) + EPS)
        g = r * jnp.dot(h, wg_scr[...], preferred_element_type=F32)
        u = r * jnp.dot(h, wu_scr[...], preferred_element_type=F32)
        o_ref[rows, :] = (g * jax.nn.sigmoid(g) * u).astype(o_ref.dtype)


def swiglu_in(xg, ssq, w, layer, bm=2048, bn=256):
    m, k = xg.shape
    f = w.shape[2] // 2
    bm = min(bm, m)
    nj = f // bn
    blk = (2 * _nbytes((k, bn), F32) + _nbytes((bm, k), BF16) + _nbytes((bm, V7X_LANES), F32)
           + _nbytes((bm, bn), BF16))
    scratch = 2 * _nbytes((k, bn), BF16)
    return pl.pallas_call(
        _swiglu_kernel,
        grid=(nj, m // bm),
        in_specs=[pl.BlockSpec((bm, k), lambda j, i: (i, 0)),
                  pl.BlockSpec((bm, V7X_LANES), lambda j, i: (i, 0)),
                  pl.BlockSpec((None, k, bn), lambda j, i: (layer, 0, j)),
                  pl.BlockSpec((None, k, bn), lambda j, i: (layer, 0, j + nj))],
        out_specs=pl.BlockSpec((bm, bn), lambda j, i: (i, j)),
        out_shape=jax.ShapeDtypeStruct((m, f), BF16),
        scratch_shapes=[pltpu.VMEM((k, bn), BF16), pltpu.VMEM((k, bn), BF16)],
        compiler_params=pltpu.CompilerParams(
            dimension_semantics=("arbitrary", "arbitrary"),
            vmem_limit_bytes=min(2 * blk + scratch + V7X_VMEM_INTERNAL_BYTES // 2,
                                 V7X_VMEM_BYTES - V7X_VMEM_UNREQUESTED_BYTES)),
        name="swiglu_in",
    )(xg, ssq, w, w)


def _matmul_kernel(xg_ref, ssq_ref, w_ref, o_ref, r_scr):
    @pl.when(pl.program_id(1) == 0)
    def _():
        r_scr[...] = _row_scale(ssq_ref, xg_ref.shape[1])

    w = w_ref[...].astype(BF16)
    for rows in _row_blocks(xg_ref.shape[0]):
        y = r_scr[rows, :] * jnp.dot(xg_ref[rows, :], w, preferred_element_type=F32)
        o_ref[rows, :] = y.astype(o_ref.dtype)


def matmul(xg, ssq, w, layer, out_dtype, bm=2048, bn=512):
    m, k = xg.shape
    n = w.shape[2]
    bm, bn = min(bm, m), min(bn, n)
    blk = _nbytes((k, bn), F32) + _nbytes((bm, bn), out_dtype)
    tmp = _nbytes((k, bn), BF16) + _nbytes((bm, bn), F32) + _nbytes((bm, V7X_LANES), F32)
    single = _nbytes((bm, k), BF16) + _nbytes((bm, V7X_LANES), F32)
    return pl.pallas_call(
        _matmul_kernel,
        grid=(m // bm, n // bn),
        in_specs=[_resident((bm, k), lambda i, j: (i, 0)),
                  _resident((bm, V7X_LANES), lambda i, j: (i, 0)),
                  _weight(k, bn, layer, lambda j: j)],
        out_specs=pl.BlockSpec((bm, bn), lambda i, j: (i, j)),
        out_shape=jax.ShapeDtypeStruct((m, n), out_dtype),
        scratch_shapes=[pltpu.VMEM((bm, 1), F32)],
        compiler_params=_params(blk, tmp, ("arbitrary", "arbitrary"), single),
        name="matmul",
    )(xg, ssq, w)


def _matmul_res_kernel(a_ref, w_ref, x_ref, o_ref, *, scale):
    w = w_ref[...].astype(BF16)
    for rows in _row_blocks(a_ref.shape[0]):
        o_ref[rows, :] = x_ref[rows, :] + scale * jnp.dot(a_ref[rows, :], w, preferred_element_type=F32)


def _matmul_res_norm_kernel(a_ref, w_ref, x_ref, g_ref, o_ref, xg_ref, ssq_ref, *, scale):
    @pl.when(pl.program_id(1) == 0)
    def _():
        ssq_ref[...] = jnp.zeros(ssq_ref.shape, F32)

    w = w_ref[...].astype(BF16)
    for rows in _row_blocks(a_ref.shape[0]):
        y = x_ref[rows, :] + scale * jnp.dot(a_ref[rows, :], w, preferred_element_type=F32)
        o_ref[rows, :] = y
        xg_ref[rows, :] = (y * g_ref[...]).astype(xg_ref.dtype)
        ssq_ref[rows, :] += _lane_partial_sq(y)


def matmul_res(a, w, layer, x, scale, bm, next_gain=None, bn=256):
    m, k = a.shape
    n = w.shape[2]
    bm, bn = min(bm, m), min(bn, n)
    tile = pl.BlockSpec((bm, bn), lambda i, j: (i, j))
    in_specs = [_resident((bm, k), lambda i, j: (i, 0)), _weight(k, bn, layer, lambda j: j), tile]
    blk = _nbytes((k, bn), F32) + 2 * _nbytes((bm, bn), F32)
    tmp = _nbytes((k, bn), BF16) + _nbytes((bm, bn), F32)
    single = _nbytes((bm, k), a.dtype)
    sem = ("arbitrary", "arbitrary")
    if next_gain is None:
        return pl.pallas_call(
            functools.partial(_matmul_res_kernel, scale=scale),
            grid=(m // bm, n // bn),
            in_specs=in_specs,
            out_specs=tile,
            out_shape=jax.ShapeDtypeStruct((m, n), F32),
            compiler_params=_params(blk, tmp, sem, single),
            name="matmul_res",
        )(a, w, x)
    blk += _nbytes((1, bn), F32) + _nbytes((bm, bn), BF16) + _nbytes((bm, V7X_LANES), F32)
    return pl.pallas_call(
        functools.partial(_matmul_res_norm_kernel, scale=scale),
        grid=(m // bm, n // bn),
        in_specs=in_specs + [pl.BlockSpec((1, bn), lambda i, j: (0, j))],
        out_specs=[tile, tile, pl.BlockSpec((bm, V7X_LANES), lambda i, j: (i, 0))],
        out_shape=[jax.ShapeDtypeStruct((m, n), F32), jax.ShapeDtypeStruct((m, n), BF16),
                   jax.ShapeDtypeStruct((m, V7X_LANES), F32)],
        compiler_params=_params(blk, tmp, sem, single),
        name="matmul_res_norm",
    )(a, w, x, next_gain.astype(F32).reshape(1, n))


def _glu_kernel(a_ref, wa_ref, wb_ref, o_ref):
    wa = wa_ref[...].astype(BF16)
    wb = wb_ref[...].astype(BF16)
    for rows in _row_blocks(a_ref.shape[0]):
        a = a_ref[rows, :].astype(BF16)
        za = jnp.dot(a, wa, preferred_element_type=F32)
        zb = jnp.dot(a, wb, preferred_element_type=F32)
        o_ref[rows, :] = (za * jax.nn.sigmoid(zb)).astype(o_ref.dtype)


def glu_matmul(a, w, layer, bm=2048, bn=256):
    m, k = a.shape
    n = w.shape[2] // 2
    bm = min(bm, m)
    nj = n // bn
    blk = 2 * _nbytes((k, bn), F32) + _nbytes((bm, bn), BF16)
    tmp = _nbytes((bm, k), BF16) + 2 * _nbytes((k, bn), BF16) + 2 * _nbytes((bm, bn), F32)
    return pl.pallas_call(
        _glu_kernel,
        grid=(m // bm, nj),
        in_specs=[_resident((bm, k), lambda i, j: (i, 0)),
                  _weight(k, bn, layer, lambda j: j),
                  _weight(k, bn, layer, lambda j: j + nj)],
        out_specs=pl.BlockSpec((bm, bn), lambda i, j: (i, j)),
        out_shape=jax.ShapeDtypeStruct((m, n), BF16),
        compiler_params=_params(blk, tmp, ("arbitrary", "arbitrary"), _nbytes((bm, k), a.dtype)),
        name="glu_matmul",
    )(a, w, w)


LOG2E = math.log2(math.e)


def _segment_mean_sq(x, seg_ref):
    sq = x * x
    hi = sq.astype(BF16)
    lo = (sq - hi.astype(F32)).astype(BF16)
    s = (jnp.dot(hi, seg_ref[...], preferred_element_type=F32)
         + jnp.dot(lo, seg_ref[...], preferred_element_type=F32))
    return s * (1.0 / HEAD_DIM)


def _attn_kernel(sink_ref, q_ref, kp_ref, kc_ref, vp_ref, vc_ref, bias_ref, qg_ref, kg_ref,
                 seg_ref, o_ref, s_scr, p_scr, sink_scr):
    lanes = V7X_LANES
    low = lax.broadcasted_iota(jnp.int32, (1, lanes), 1) < HEAD_DIM
    high = jnp.logical_not(low)

    kk = jnp.concatenate([kp_ref[...], kc_ref[...]], axis=0)
    vv = jnp.concatenate([vp_ref[...], vc_ref[...]], axis=0)
    ones = jnp.ones((HEAD_DIM, 2 * BLOCK), F32)

    k_low, k_high, v_low, v_high = [], [], [], []
    for c in range(KV_WIDTH // lanes):
        kc = kk[:, c * lanes:(c + 1) * lanes]
        kc = kc * lax.rsqrt(_segment_mean_sq(kc, seg_ref) + EPS) * kg_ref[:, c * lanes:(c + 1) * lanes]
        ks = pltpu.roll(kc, HEAD_DIM, axis=1)
        vt = vv[:, c * lanes:(c + 1) * lanes].T
        for odd in range(2):
            k_low.append(jnp.where(low, ks if odd else kc, 0.0).astype(BF16))
            k_high.append(jnp.where(high, kc if odd else ks, 0.0).astype(BF16))
            vth = vt[odd * HEAD_DIM:(odd + 1) * HEAD_DIM, :]
            v_low.append(jnp.concatenate([vth, ones], axis=0).astype(BF16))
            v_high.append(jnp.concatenate([ones, vth], axis=0).astype(BF16))

    n_pairs = ATTN_WIDTH // lanes
    for pair in range(n_pairs):
        kv = pair // (Q_GROUP // 2)
        qt = q_ref[:, pair * lanes:(pair + 1) * lanes].T
        ms = [jnp.mean(jnp.square(qt[h * HEAD_DIM:(h + 1) * HEAD_DIM]), axis=0, keepdims=True)
              for h in range(2)]
        rs = jnp.concatenate([jnp.broadcast_to(lax.rsqrt(m + EPS), (HEAD_DIM, BLOCK)) for m in ms], axis=0)
        qt = (qt * rs * qg_ref[...]).astype(BF16)
        for half, kh in enumerate((k_low[kv], k_high[kv])):
            head = 2 * pair + half
            s_scr[head] = jnp.dot(kh, qt, preferred_element_type=F32) + bias_ref[head]

    for head in range(N_Q_HEADS):
        s = s_scr[head]
        sink = sink_ref[head]
        mx = jnp.maximum(jnp.max(s, axis=0, keepdims=True), sink)
        p_scr[head] = jnp.exp2(s - mx).astype(BF16)
        sink_scr[head:head + 1, :] = jnp.exp2(sink - mx)

    for pair in range(n_pairs):
        kv = pair // (Q_GROUP // 2)
        num, den = [], []
        for half, vh in enumerate((v_low[kv], v_high[kv])):
            head = 2 * pair + half
            o = jnp.dot(vh, p_scr[head], preferred_element_type=F32)
            lo_rows, hi_rows = o[:HEAD_DIM], o[HEAD_DIM:]
            num.append(lo_rows if half == 0 else hi_rows)
            den.append((hi_rows if half == 0 else lo_rows) + sink_scr[head:head + 1, :])
        ot = jnp.concatenate(num, axis=0) / jnp.concatenate(den, axis=0)
        o_ref[:, pair * lanes:(pair + 1) * lanes] = ot.T.astype(o_ref.dtype)


def attention(proj, bias, q_gain, k_gain, sinks, seq):
    m = proj.shape[0]
    nb = seq // BLOCK
    nblk = m // BLOCK
    lanes = V7X_LANES
    seg_id = jnp.arange(lanes) // HEAD_DIM
    seg = (seg_id[:, None] == seg_id[None, :]).astype(BF16)
    qg = jnp.broadcast_to(jnp.tile(q_gain.astype(F32) * (HEAD_DIM ** -0.5 * LOG2E), 2)[:, None], (lanes, BLOCK))
    kg = jnp.tile(k_gain.astype(F32), N_KV_HEADS).reshape(1, KV_WIDTH)
    kcol, vcol = OFF_K // KV_WIDTH, OFF_V // KV_WIDTH
    table = (None,) + bias.shape[1:]

    def prev(i):
        return jnp.where(i % nb == 0, i, i - 1)

    blk = (_nbytes((BLOCK, ATTN_WIDTH), F32) + 4 * _nbytes((BLOCK, KV_WIDTH), F32)
           + _nbytes(bias.shape[1:], F32) + _nbytes((BLOCK, ATTN_WIDTH), BF16))
    scratch = (_nbytes((N_Q_HEADS, 2 * BLOCK, BLOCK), F32) + _nbytes((N_Q_HEADS, 2 * BLOCK, BLOCK), BF16)
               + _nbytes((N_Q_HEADS, BLOCK), F32))
    return pl.pallas_call(
        _attn_kernel,
        grid=(nblk,),
        in_specs=[pl.BlockSpec(memory_space=pltpu.SMEM),
                  pl.BlockSpec((BLOCK, ATTN_WIDTH), lambda i: (i, 0)),
                  pl.BlockSpec((BLOCK, KV_WIDTH), lambda i: (prev(i), kcol)),
                  pl.BlockSpec((BLOCK, KV_WIDTH), lambda i: (i, kcol)),
                  pl.BlockSpec((BLOCK, KV_WIDTH), lambda i: (prev(i), vcol)),
                  pl.BlockSpec((BLOCK, KV_WIDTH), lambda i: (i, vcol)),
                  pl.BlockSpec(table, lambda i: (jnp.where(i % nb == 0, 0, 1), 0, 0, 0)),
                  pl.BlockSpec((lanes, BLOCK), lambda i: (0, 0)),
                  pl.BlockSpec((1, KV_WIDTH), lambda i: (0, 0)),
                  pl.BlockSpec((lanes, lanes), lambda i: (0, 0))],
        out_specs=pl.BlockSpec((BLOCK, ATTN_WIDTH), lambda i: (i, 0)),
        out_shape=jax.ShapeDtypeStruct((m, ATTN_WIDTH), BF16),
        scratch_shapes=[pltpu.VMEM((N_Q_HEADS, 2 * BLOCK, BLOCK), F32),
                        pltpu.VMEM((N_Q_HEADS, 2 * BLOCK, BLOCK), BF16),
                        pltpu.VMEM((N_Q_HEADS, BLOCK), F32)],
        compiler_params=_params(blk, scratch, semantics=("arbitrary",)),
        name="swa_attention",
    )(sinks.astype(F32) * LOG2E, proj, proj, proj, proj, proj, bias, qg, kg, seg)


def _t5_bucket(dist):
    max_exact = N_BUCKETS // 2
    n = jnp.maximum(dist, 0)
    ratio = jnp.log(jnp.maximum(n, 1).astype(F32) / max_exact) / math.log(MAX_DISTANCE / max_exact)
    large = jnp.minimum(max_exact + (ratio * (N_BUCKETS - max_exact)).astype(jnp.int32), N_BUCKETS - 1)
    return jnp.where(n < max_exact, n, large)


def band_bias(rel_bias):
    keys, period = 2 * BLOCK, 4 * BLOCK
    by_dist = rel_bias.astype(F32)[_t5_bucket(jnp.arange(WINDOW))].T * LOG2E
    v = jnp.concatenate([by_dist, jnp.full((N_Q_HEADS, period - WINDOW), -jnp.inf, F32)], axis=1)
    a = jnp.tile(v, (1, keys))[:, :keys * (period - 1)].reshape(N_Q_HEADS, keys, period - 1)
    rest = a[:, :, BLOCK:2 * BLOCK]
    c = jnp.arange(keys)[None, :, None]
    first = jnp.where(c >= BLOCK, rest, -jnp.inf)
    return jnp.stack([first, rest])


CONV_HALO = 32
CONV_ROWS = 32


def _conv_kernel(a_ref, g_ref, w_ref, b_ref, lg_ref, lb_ref, pw_ref, o_ref, ubuf, shbuf, ybuf, *, nt, tt):
    t = pl.program_id(0) % nt

    @pl.when(t == 0)
    def _():
        ubuf[0:CONV_HALO, :] = jnp.zeros((CONV_HALO, CONV_WIDTH), F32)

    @pl.when(t != 0)
    def _():
        ubuf[0:CONV_HALO, :] = ubuf[tt:tt + CONV_HALO, :]

    ubuf[CONV_HALO:CONV_HALO + tt, :] = a_ref[...] * jax.nn.sigmoid(g_ref[...])

    first = CONV_HALO - (CONV_KERNEL - 1)
    cw = 2 * V7X_LANES

    span = tt + CONV_HALO - V7X_SUBLANES
    for r in range(1, V7X_SUBLANES):
        shbuf[r - 1, 0:span, :] = ubuf[r:r + span, :]

    def rows(rb, carry):
        base = pl.multiple_of(rb * CONV_ROWS, CONV_ROWS)
        for cb in range(CONV_WIDTH // cw):
            cols = slice(cb * cw, (cb + 1) * cw)
            acc = jnp.zeros((CONV_ROWS, cw), F32) + b_ref[:, cols]
            for j in range(CONV_KERNEL):
                r, a = (first + j) % V7X_SUBLANES, (first + j) // V7X_SUBLANES * V7X_SUBLANES
                tap = (ubuf[pl.ds(base + a, CONV_ROWS), cols] if r == 0
                       else shbuf[r - 1, pl.ds(base + a, CONV_ROWS), cols])
                acc = acc + w_ref[j:j + 1, cols] * tap
            ybuf[pl.ds(base, CONV_ROWS), cols] = acc
        return carry

    lax.fori_loop(0, tt // CONV_ROWS, rows, 0)

    y = ybuf[...]
    mu = jnp.mean(y, axis=-1, keepdims=True)
    yc = y - mu
    var = jnp.mean(yc * yc, axis=-1, keepdims=True)
    z = yc * lax.rsqrt(var + EPS) * lg_ref[...] + lb_ref[...]
    z = z * jax.nn.sigmoid(z)
    o_ref[...] = _wdot(z.astype(BF16), pw_ref).astype(o_ref.dtype)


def conv_branch(proj, w_dw, b_dw, ln_g, ln_b, w_pw, layer, seq, tt=512):
    m = proj.shape[0]
    tt = min(tt, seq)
    nt = seq // tt
    acol = OFF_CONV // CONV_WIDTH
    w_pad = jnp.zeros((CONV_HALO, CONV_WIDTH), F32).at[:CONV_KERNEL].set(w_dw.astype(F32))
    row = lambda v: v.astype(F32).reshape(1, CONV_WIDTH)
    blk = (2 * _nbytes((tt, CONV_WIDTH), F32) + _nbytes((CONV_HALO, CONV_WIDTH), F32)
           + 3 * _nbytes((1, CONV_WIDTH), F32) + _nbytes((CONV_WIDTH, CONV_WIDTH), F32)
           + _nbytes((tt, CONV_WIDTH), BF16))
    scratch = V7X_SUBLANES * _nbytes((tt + CONV_HALO, CONV_WIDTH), F32) + _nbytes((tt, CONV_WIDTH), F32)
    return pl.pallas_call(
        functools.partial(_conv_kernel, nt=nt, tt=tt),
        grid=(m // tt,),
        in_specs=[pl.BlockSpec((tt, CONV_WIDTH), lambda i: (i, acol)),
                  pl.BlockSpec((tt, CONV_WIDTH), lambda i: (i, acol + 1)),
                  pl.BlockSpec((CONV_HALO, CONV_WIDTH), lambda i: (0, 0)),
                  pl.BlockSpec((1, CONV_WIDTH), lambda i: (0, 0)),
                  pl.BlockSpec((1, CONV_WIDTH), lambda i: (0, 0)),
                  pl.BlockSpec((1, CONV_WIDTH), lambda i: (0, 0)),
                  pl.BlockSpec((None, CONV_WIDTH, CONV_WIDTH), lambda i: (layer, 0, 0))],
        out_specs=pl.BlockSpec((tt, CONV_WIDTH), lambda i: (i, 0)),
        out_shape=jax.ShapeDtypeStruct((m, CONV_WIDTH), BF16),
        scratch_shapes=[pltpu.VMEM((tt + CONV_HALO, CONV_WIDTH), F32),
                        pltpu.VMEM((V7X_SUBLANES - 1, tt + CONV_HALO, CONV_WIDTH), F32),
                        pltpu.VMEM((tt, CONV_WIDTH), F32)],
        compiler_params=_params(blk, scratch, semantics=("arbitrary",)),
        name="conv_branch",
    )(proj, proj, w_pad, row(b_dw), row(ln_g), row(ln_b), w_pw)


def _swap_re_im(h):
    lanes = V7X_LANES
    return jnp.concatenate([pltpu.roll(h[:, k * lanes:(k + 1) * lanes], SSM_STATE, axis=1)
                            for k in range(h.shape[1] // lanes)], axis=1)


def _expand_operands(e_ref, f_ref, ws_scr, wi_scr, wxt_scr):
    T, rows_q, lanes = SSM_CHUNK, SSM_QCH, V7X_LANES
    wide = SSM_QGROUPS * lanes
    r = lax.broadcasted_iota(jnp.int32, (rows_q, wide), 0)
    c = lax.broadcasted_iota(jnp.int32, (rows_q, wide), 1)
    own_slab = (r // SSM_GROUP) == (c // lanes)
    r2 = lax.broadcasted_iota(jnp.int32, (rows_q, rows_q), 0)
    c2 = lax.broadcasted_iota(jnp.int32, (rows_q, rows_q), 1)
    same_group = (r2 // SSM_GROUP) == (c2 // SSM_GROUP)

    def group_diag(blk):
        return jnp.where(own_slab, jnp.concatenate([blk] * SSM_QGROUPS, axis=1), 0.0).astype(BF16)

    for s in range(T):
        ws_scr[s * rows_q:(s + 1) * rows_q, :] = group_diag(e_ref[0, T - 1 - s])
        wxt_scr[s * rows_q:(s + 1) * rows_q, :] = group_diag(f_ref[0, s + 1])
    f0 = f_ref[0, 0]
    zero = jnp.zeros((rows_q, rows_q), BF16)
    kbd = []
    for l in range(T):
        k_l = lax.dot_general(e_ref[0, l], f0, (((1,), (1,)), ((), ())), preferred_element_type=F32,
                              precision=lax.Precision.HIGHEST)
        kbd.append(jnp.where(same_group, k_l, 0.0).astype(BF16))
    for s in range(T):
        for t in range(T):
            wi_scr[s * rows_q:(s + 1) * rows_q, t * rows_q:(t + 1) * rows_q] = kbd[t - s] if t >= s else zero


def _ssm_kernel(u_ref, e_ref, f_ref, a1_ref, a2_ref, d_ref, o_ref, ws_scr, wi_scr, wxt_scr, *, nc):
    @pl.when(pl.program_id(1) == 0)
    def _():
        _expand_operands(e_ref, f_ref, ws_scr, wi_scr, wxt_scr)

    x = jnp.concatenate([u_ref[pl.ds(t, nc, stride=SSM_CHUNK), :] for t in range(SSM_CHUNK)], axis=1)
    xb = x.astype(BF16)
    h = jnp.dot(xb, ws_scr[...], preferred_element_type=F32)
    row = lax.broadcasted_iota(jnp.int32, (nc, 1), 0)
    level = 0
    while (1 << level) < nc:
        d = 1 << level
        a1 = a1_ref[0, level:level + 1, :]
        a2 = a2_ref[0, level:level + 1, :]
        if d % V7X_SUBLANES:
            prev = jnp.where(row >= d, pltpu.roll(h, d, axis=0), 0.0)
            h = h + a1 * prev + a2 * _swap_re_im(prev)
        else:
            prev = h[:nc - d]
            h = jnp.concatenate([h[:d], h[d:] + a1 * prev + a2 * _swap_re_im(prev)], axis=0)
        level += 1
    hp = jnp.where(row >= 1, pltpu.roll(h, 1, axis=0), 0.0).astype(BF16)
    y = (jnp.dot(xb, wi_scr[...], preferred_element_type=F32)
         + lax.dot_general(hp, wxt_scr[...], (((1,), (1,)), ((), ())), preferred_element_type=F32)
         + d_ref[0] * x)
    y = jax.nn.gelu(y)
    for t in range(SSM_CHUNK):
        o_ref[pl.ds(t, nc, stride=SSM_CHUNK), :] = y[:, t * SSM_QCH:(t + 1) * SSM_QCH]


def ssm_branch(proj, operands, d_skip, layer, seq):
    e_blk, f_blk, a1, a2 = operands
    m = proj.shape[0]
    nseq = m // seq
    nc = seq // SSM_CHUNK
    ucol = OFF_SSM // SSM_QCH
    kw = SSM_CHUNK * SSM_QCH
    sw = 2 * SSM_QSTATE
    nlev = a1.shape[2]
    d_t = jnp.tile(d_skip.astype(F32).reshape(-1, SSM_NQ, 1, SSM_QCH), (1, 1, 1, SSM_CHUNK))
    blk = (2 * _nbytes((seq, SSM_QCH), F32) + _nbytes(e_blk.shape[2:], F32) + _nbytes(f_blk.shape[2:], F32)
           + 2 * _nbytes((nlev, sw), F32) + _nbytes((1, kw), F32))
    scratch = 2 * _nbytes((kw, sw), BF16) + _nbytes((kw, kw), BF16)
    tmp = 6 * _nbytes((nc, kw), F32)

    def per_q(shape):
        return pl.BlockSpec((None, 1) + shape, lambda q, b: (layer, q) + (0,) * len(shape))

    return pl.pallas_call(
        functools.partial(_ssm_kernel, nc=nc),
        grid=(SSM_NQ, nseq),
        in_specs=[pl.BlockSpec((seq, SSM_QCH), lambda q, b: (b, ucol + q)),
                  per_q(e_blk.shape[2:]), per_q(f_blk.shape[2:]),
                  per_q((nlev, sw)), per_q((nlev, sw)), per_q((1, kw))],
        out_specs=pl.BlockSpec((seq, SSM_QCH), lambda q, b: (b, q)),
        out_shape=jax.ShapeDtypeStruct((m, SSM_WIDTH), F32),
        scratch_shapes=[pltpu.VMEM((kw, sw), BF16), pltpu.VMEM((kw, kw), BF16), pltpu.VMEM((kw, sw), BF16)],
        compiler_params=_params(blk, scratch + tmp, semantics=("arbitrary", "arbitrary")),
        name="ssm_scan",
    )(proj, e_blk, f_blk, a1, a2, d_t)


def _cmul(ar, ai, br, bi):
    return ar * br - ai * bi, ar * bi + ai * br


def ssm_operands(a_re, a_im, log_dt, b_re, b_im, c_re, c_im, n_chunks):
    f32 = F32
    dt = jnp.exp(log_dt.astype(f32))[:, None]
    lam_re, lam_im = a_re.astype(f32), a_im.astype(f32)
    mag = jnp.exp(dt * lam_re)
    ang = dt * lam_im
    lb_re, lb_im = mag * jnp.cos(ang), mag * jnp.sin(ang)
    nr = lb_re - 1.0
    den = lam_re * lam_re + lam_im * lam_im
    coef_re = (nr * lam_re + lb_im * lam_im) / den
    coef_im = (lb_im * lam_re - nr * lam_im) / den
    br, bi = b_re.astype(f32), b_im.astype(f32)
    bb_re = coef_re[..., None] * br - coef_im[..., None] * bi
    bb_im = coef_re[..., None] * bi + coef_im[..., None] * br
    cr, ci = c_re.astype(f32), c_im.astype(f32)

    pw = [(jnp.ones_like(lb_re), jnp.zeros_like(lb_im))]
    for _ in range(SSM_CHUNK):
        pw.append(_cmul(pw[-1][0], pw[-1][1], lb_re, lb_im))

    T, Q, C, P = SSM_CHUNK, SSM_QGROUPS, SSM_GROUP, SSM_STATE
    rows_q = Q * C
    pw_re = jnp.stack([p[0] for p in pw])
    pw_im = jnp.stack([p[1] for p in pw])

    def per_tile(blocks):
        k = blocks.shape[0]
        return blocks.reshape(k, SSM_NQ, rows_q, 2 * P).transpose(1, 0, 2, 3)

    e_re, e_im = _cmul(pw_re[:T, :, :, None], pw_im[:T, :, :, None], bb_re, bb_im)
    e_blk = per_tile(jnp.concatenate([jnp.swapaxes(e_re, 2, 3), jnp.swapaxes(e_im, 2, 3)], axis=-1))
    f_re, f_im = _cmul(cr, ci, pw_re[:, :, None, :], pw_im[:, :, None, :])
    f_blk = per_tile(jnp.concatenate([f_re, -f_im], axis=-1))

    a = pw[T]
    a1, a2 = [], []
    level = 0
    while (1 << level) < n_chunks:
        a1.append(jnp.concatenate([a[0], a[0]], axis=-1).reshape(SSM_NQ, Q * 2 * P))
        a2.append(jnp.concatenate([-a[1], a[1]], axis=-1).reshape(SSM_NQ, Q * 2 * P))
        a = _cmul(a[0], a[1], a[0], a[1])
        level += 1
    return e_blk, f_blk, jnp.stack(a1, axis=1), jnp.stack(a2, axis=1)


def _merge_kernel(oa_ref, oc_ref, os_ref, ga_ref, gc_ref, gs_ref, w_ref, o_ref):
    wa = w_ref[0:ATTN_WIDTH, :].astype(BF16)
    wc = w_ref[ATTN_WIDTH:ATTN_WIDTH + CONV_WIDTH, :].astype(BF16)
    ws = w_ref[ATTN_WIDTH + CONV_WIDTH:MIX_WIDTH, :].astype(BF16)
    for rows in _row_blocks(o_ref.shape[0]):
        ya = jnp.dot(oa_ref[rows, :], wa, preferred_element_type=F32)
        yc = jnp.dot(oc_ref[rows, :], wc, preferred_element_type=F32)
        ys = jnp.dot(os_ref[rows, :], ws, preferred_element_type=F32)
        merged = (jax.nn.sigmoid(ga_ref[rows, :]) * ya + jax.nn.sigmoid(gc_ref[rows, :]) * yc
                  + jax.nn.sigmoid(gs_ref[rows, :]) * ys)
        o_ref[rows, :] = merged.astype(o_ref.dtype)


def branch_merge(o_attn, o_conv, o_ssm, proj, w_branch, layer, bm=2048, bn=256):
    m = proj.shape[0]
    bm = min(bm, m)
    gcol = OFF_GATE // bn
    gstep = D_MODEL // bn
    blk = 3 * _nbytes((bm, bn), F32) + _nbytes((MIX_WIDTH, bn), F32) + _nbytes((bm, bn), BF16)
    tmp = _nbytes((MIX_WIDTH, bn), BF16) + 4 * _nbytes((bm, bn), F32)
    return pl.pallas_call(
        _merge_kernel,
        grid=(m // bm, D_MODEL // bn),
        in_specs=[_resident((bm, ATTN_WIDTH), lambda i, j: (i, 0)),
                  _resident((bm, CONV_WIDTH), lambda i, j: (i, 0)),
                  _resident((bm, SSM_WIDTH), lambda i, j: (i, 0)),
                  pl.BlockSpec((bm, bn), lambda i, j: (i, gcol + j)),
                  pl.BlockSpec((bm, bn), lambda i, j: (i, gcol + gstep + j)),
                  pl.BlockSpec((bm, bn), lambda i, j: (i, gcol + 2 * gstep + j)),
                  _weight(MIX_WIDTH, bn, layer, lambda j: j)],
        out_specs=pl.BlockSpec((bm, bn), lambda i, j: (i, j)),
        out_shape=jax.ShapeDtypeStruct((m, D_MODEL), BF16),
        compiler_params=_params(blk, tmp, ("arbitrary", "arbitrary"), _nbytes((bm, MIX_WIDTH), BF16)),
        name="branch_merge",
    )(o_attn, o_conv, o_ssm, proj, proj, proj, w_branch)


FFN_OUT_ROWS = 1024


def _ffn(x, normed, w_in, w_out, layer, next_gain):
    act = swiglu_in(*normed, w_in, layer)
    return matmul_res(act, w_out, layer, x, 0.5, bm=FFN_OUT_ROWS, next_gain=next_gain)


def _layer(x, normed, seq, bias, ssm_ops, p, layer):
    at = lambda name: p[name][layer]
    x, *normed = _ffn(x, normed, p['w_ffn1_in'], p['w_ffn1_out'], layer, at('mix_norm'))
    proj = matmul(*normed, p['w_in'], layer, F32)
    o_attn = attention(proj, bias, at('q_norm'), at('k_norm'), at('attn_sinks'), seq)
    o_conv = conv_branch(proj, at('conv_dw'), at('conv_dw_bias'), at('conv_ln_g'), at('conv_ln_b'),
                         p['conv_pw'], layer, seq)
    y_ssm = ssm_branch(proj, ssm_ops, p['ssm_d'], layer, seq)
    o_ssm = glu_matmul(y_ssm, p['ssm_glu'], layer)
    merged = branch_merge(o_attn, o_conv, o_ssm, proj, p['w_branch'], layer)
    x, *normed = matmul_res(merged, p['w_out'], layer, x, 1.0, bm=2048, next_gain=at('ffn2_norm'))
    if layer + 1 == DEPTH:
        return _ffn(x, normed, p['w_ffn2_in'], p['w_ffn2_out'], layer, None), None
    x, *normed = _ffn(x, normed, p['w_ffn2_in'], p['w_ffn2_out'], layer, p['ffn1_norm'][layer + 1])
    return x, normed


def kernel(x, rel_bias, ffn1_norm, w_ffn1_in, w_ffn1_out, mix_norm, w_in, q_norm, k_norm, attn_sinks,
           conv_dw, conv_dw_bias, conv_ln_g, conv_ln_b, conv_pw, ssm_a_re, ssm_a_im, ssm_log_dt,
           ssm_b_re, ssm_b_im, ssm_c_re, ssm_c_im, ssm_d, ssm_glu, w_branch, w_out, ffn2_norm,
           w_ffn2_in, w_ffn2_out):
    layer_params = dict(
        ffn1_norm=ffn1_norm, w_ffn1_in=w_ffn1_in, w_ffn1_out=w_ffn1_out, mix_norm=mix_norm, w_in=w_in,
        q_norm=q_norm, k_norm=k_norm, attn_sinks=attn_sinks, conv_dw=conv_dw, conv_dw_bias=conv_dw_bias,
        conv_ln_g=conv_ln_g, conv_ln_b=conv_ln_b, conv_pw=conv_pw, ssm_a_re=ssm_a_re, ssm_a_im=ssm_a_im,
        ssm_log_dt=ssm_log_dt, ssm_b_re=ssm_b_re, ssm_b_im=ssm_b_im, ssm_c_re=ssm_c_re, ssm_c_im=ssm_c_im,
        ssm_d=ssm_d, ssm_glu=ssm_glu, w_branch=w_branch, w_out=w_out, ffn2_norm=ffn2_norm,
        w_ffn2_in=w_ffn2_in, w_ffn2_out=w_ffn2_out)
    b, seq, d = x.shape
    bias = band_bias(rel_bias)
    ssm_ops = jax.vmap(functools.partial(ssm_operands, n_chunks=seq // SSM_CHUNK))(
        ssm_a_re, ssm_a_im, ssm_log_dt, ssm_b_re, ssm_b_im, ssm_c_re, ssm_c_im)
    xf = x.reshape(b * seq, d)
    normed = norm_prep(xf, ffn1_norm[0])
    for layer in range(DEPTH):
        xf, normed = _layer(xf, normed, seq, bias, ssm_ops, layer_params, layer)
    return xf.reshape(b, seq, d)
```
